```python
import math
import jax, jax.numpy as jnp
from jax import lax
import numpy as np

D_MODEL = 1024
BATCH = 16
SEQ = 2048
DEPTH = 4

GRID_W = 64
CTX_LEN = 256
N_MIXERS = 3
N_LAYERS_A = (DEPTH + 2) // 3
N_LAYERS_B = (DEPTH + 1) // 3
N_LAYERS_C = DEPTH // 3
HEAD_DIM = 64
DIFF_HEADS = D_MODEL // (2 * HEAD_DIM)
DIFF_V_DIM = 2 * HEAD_DIM
GQA_HEADS = D_MODEL // HEAD_DIM
GQA_KV_HEADS = GQA_HEADS // 4
GQA_GROUP = GQA_HEADS // GQA_KV_HEADS
Q_BLOCK = 128
WINDOW = 128
D_FF = 256 * ((8 * D_MODEL // 3 + 255) // 256)
CONV_W = 3
ROPE_THETA = 10000.0
EPS = 1e-6
ATTN_SCALE = HEAD_DIM ** -0.5

kernel_name = 'hybrid_diff_grid_window_convffn_trunk'


def rms_norm(x, g):
    xf = x.astype(jnp.float32)
    y = xf * lax.rsqrt(jnp.mean(xf * xf, axis=-1, keepdims=True) + EPS)
    return (y * g.astype(jnp.float32)).astype(x.dtype)


def modulate(h, shift, scale):
    return h * (1.0 + scale) + shift


def grid_rope_tables(rows):
    t = jnp.arange(rows * GRID_W)
    row = (t // GRID_W).astype(jnp.float32)
    col = (t % GRID_W).astype(jnp.float32)
    n_freq = HEAD_DIM // 4
    inv_freq = ROPE_THETA ** (-jnp.arange(n_freq, dtype=jnp.float32) / n_freq)
    ang = jnp.concatenate([row[:, None] * inv_freq, col[:, None] * inv_freq], axis=-1)
    return jnp.cos(ang), jnp.sin(ang)


def apply_rope(x, cos, sin):
    half = x.shape[-1] // 2
    shape = (1, cos.shape[0]) + (1,) * (x.ndim - 3) + (cos.shape[1],)
    cs = cos.reshape(shape).astype(x.dtype)
    sn = sin.reshape(shape).astype(x.dtype)
    x1, x2 = x[..., :half], x[..., half:]
    return jnp.concatenate([x1 * cs - x2 * sn, x2 * cs + x1 * sn], axis=-1)


def to_blocks(t):
    b, l = t.shape[:2]
    t = t.reshape((b, l // Q_BLOCK, Q_BLOCK) + t.shape[2:])
    return jnp.moveaxis(t, 1, 0)


def from_blocks(t):
    t = jnp.moveaxis(t, 0, 1)
    return t.reshape((t.shape[0], t.shape[1] * t.shape[2]) + t.shape[3:])


def diff_proj(h, w_qkv, qk_g):
    b, l, _ = h.shape
    q, k, v = jnp.split(h @ w_qkv, 3, axis=-1)
    q = rms_norm(q.reshape(b, l, DIFF_HEADS, 2, HEAD_DIM), qk_g[0])
    k = rms_norm(k.reshape(b, l, DIFF_HEADS, 2, HEAD_DIM), qk_g[1])
    v = v.reshape(b, l, DIFF_HEADS, DIFF_V_DIM)
    return q, k, v


def diff_core(q, k, v, lam):
    s = jnp.einsum('bqhcd,bshcd->bhcqs', q, k).astype(jnp.float32) * ATTN_SCALE
    p = jax.nn.softmax(s, axis=-1)
    w = (p[:, :, 0] - lam * p[:, :, 1]).astype(v.dtype)
    return jnp.einsum('bhqs,bshe->bqhe', w, v)


def diff_attention(h_lat, h_ctx, w_qkv, qk_g, lam_p, head_g, w_o, lam_init, cos, sin, need_ctx):
    q_l, k_l, v_l = diff_proj(h_lat, w_qkv, qk_g)
    q_c, k_c, v_c = diff_proj(h_ctx, w_qkv, qk_g)
    q_l = apply_rope(q_l, cos, sin)
    k_l = apply_rope(k_l, cos, sin)
    lq = lam_p.astype(jnp.float32)
    lam = jnp.exp(jnp.sum(lq[0] * lq[1])) - jnp.exp(jnp.sum(lq[2] * lq[3])) + lam_init
    k_all = jnp.concatenate([k_l, k_c], axis=1)
    v_all = jnp.concatenate([v_l, v_c], axis=1)
    o_l = from_blocks(lax.map(lambda qb: diff_core(qb, k_all, v_all, lam), to_blocks(q_l)))

    def finish(o):
        b, l = o.shape[:2]
        o = rms_norm(o, head_g) * (1.0 - lam_init)
        return o.reshape(b, l, DIFF_HEADS * DIFF_V_DIM) @ w_o

    y_l = finish(o_l)
    y_c = finish(diff_core(q_c, k_c, v_c, lam)) if need_ctx else None
    return y_l, y_c


def gqa_proj(h, w_qkv, qk_g):
    b, l, _ = h.shape
    q, k, v = jnp.split(h @ w_qkv, [GQA_HEADS * HEAD_DIM, (GQA_HEADS + GQA_KV_HEADS) * HEAD_DIM], axis=-1)
    q = rms_norm(q.reshape(b, l, GQA_KV_HEADS, GQA_GROUP, HEAD_DIM), qk_g[0])
    k = rms_norm(k.reshape(b, l, GQA_KV_HEADS, HEAD_DIM), qk_g[1])
    v = v.reshape(b, l, GQA_KV_HEADS, HEAD_DIM)
    return q, k, v


def merge_heads(o):
    return o.reshape(o.shape[0], o.shape[1], GQA_HEADS * HEAD_DIM)


def gqa_dense_core(q, k, v):
    s = jnp.einsum('bqkgd,bskd->bkgqs', q, k).astype(jnp.float32) * ATTN_SCALE
    p = jax.nn.softmax(s, axis=-1).astype(v.dtype)
    return jnp.einsum('bkgqs,bskd->bqkgd', p, v)


def grid_gqa_attention(h_lat, h_ctx, w_qkv, qk_g, w_o, cos, sin, need_ctx):
    q_l, k_l, v_l = gqa_proj(h_lat, w_qkv, qk_g)
    q_c, k_c, v_c = gqa_proj(h_ctx, w_qkv, qk_g)
    q_l = apply_rope(q_l, cos, sin)
    k_l = apply_rope(k_l, cos, sin)
    k_all = jnp.concatenate([k_l, k_c], axis=1)
    v_all = jnp.concatenate([v_l, v_c], axis=1)
    o_l = from_blocks(lax.map(lambda qb: gqa_dense_core(qb, k_all, v_all), to_blocks(q_l)))
    y_l = merge_heads(o_l) @ w_o
    y_c = merge_heads(gqa_dense_core(q_c, k_c, v_c)) @ w_o if need_ctx else None
    return y_l, y_c


def window_gqa_attention(h_lat, h_ctx, w_qkv, qk_g, sink, w_o, cos, sin, need_ctx):
    q_l, k_l, v_l = gqa_proj(h_lat, w_qkv, qk_g)
    q_c, k_c, v_c = gqa_proj(h_ctx, w_qkv, qk_g)
    q_l = apply_rope(q_l, cos, sin)
    k_l = apply_rope(k_l, cos, sin)
    n_lat = h_lat.shape[1]
    n_ctx = h_ctx.shape[1]
    band = Q_BLOCK + 2 * WINDOW
    pad = ((0, 0), (WINDOW, WINDOW), (0, 0), (0, 0))
    k_pad = jnp.pad(k_l, pad)
    v_pad = jnp.pad(v_l, pad)
    sink_f = sink.astype(jnp.float32).reshape(GQA_KV_HEADS, GQA_GROUP)[None, :, :, None, None]
    q_off = jnp.arange(Q_BLOCK)[:, None]
    k_off = jnp.arange(band)[None, :]
    rel = k_off - q_off
    in_band = (rel >= 0) & (rel <= 2 * WINDOW)

    def block(args):
        i, qb = args
        kb = lax.dynamic_slice_in_dim(k_pad, i * Q_BLOCK, band, axis=1)
        vb = lax.dynamic_slice_in_dim(v_pad, i * Q_BLOCK, band, axis=1)
        kpos = i * Q_BLOCK - WINDOW + k_off
        mask = in_band & (kpos >= 0) & (kpos < n_lat)
        s_w = jnp.einsum('bqkgd,bskd->bkgqs', qb, kb).astype(jnp.float32) * ATTN_SCALE
        s_w = jnp.where(mask, s_w, -jnp.inf)
        s_c = jnp.einsum('bqkgd,bskd->bkgqs', qb, k_c).astype(jnp.float32) * ATTN_SCALE
        s_s = jnp.broadcast_to(sink_f, s_w.shape[:-1] + (1,))
        p = jax.nn.softmax(jnp.concatenate([s_w, s_c, s_s], axis=-1), axis=-1).astype(vb.dtype)
        return (jnp.einsum('bkgqs,bskd->bqkgd', p[..., :band], vb)
                + jnp.einsum('bkgqs,bskd->bqkgd', p[..., band:band + n_ctx], v_c))

    nb = n_lat // Q_BLOCK
    o_l = from_blocks(lax.map(block, (jnp.arange(nb), to_blocks(q_l))))
    y_l = merge_heads(o_l) @ w_o
    y_c = None
    if need_ctx:
        s = jnp.einsum('bqkgd,bskd->bkgqs', q_c, k_c).astype(jnp.float32) * ATTN_SCALE
        s_s = jnp.broadcast_to(sink_f, s.shape[:-1] + (1,))
        p = jax.nn.softmax(jnp.concatenate([s, s_s], axis=-1), axis=-1)[..., :n_ctx].astype(v_c.dtype)
        y_c = merge_heads(jnp.einsum('bkgqs,bskd->bqkgd', p, v_c)) @ w_o
    return y_l, y_c


def conv_ffn(h, w_up, conv_w, conv_b, w_down):
    n = h.shape[1]
    u = h @ w_up
    half = CONV_W // 2
    up = jnp.pad(u, ((0, 0), (half, half), (0, 0)))
    u = sum(up[:, j:j + n] * conv_w[j] for j in range(CONV_W)) + conv_b
    gate, val = jnp.split(u, 2, axis=-1)
    return (jax.nn.silu(gate) * val) @ w_down


def setup_inputs(seed: int = 0) -> dict:
    key = jax.random.key(seed)
    ks = iter(jax.random.split(key, 32))
    f32 = jnp.float32
    nrm = lambda shape, s: jax.random.normal(next(ks), shape, f32) * s
    gain = lambda shape: 1.0 + 0.02 * jax.random.normal(next(ks), shape, f32)
    d = D_MODEL
    qkv_gqa = (GQA_HEADS + 2 * GQA_KV_HEADS) * HEAD_DIM
    return {
        'x': nrm((BATCH, SEQ, d), 1.0),
        'c': nrm((BATCH, d), 1.0),
        'ctx': nrm((BATCH, CTX_LEN, d), 1.0),
        'c_ctx': nrm((d,), 1.0),
        'adaln_w': nrm((DEPTH, d, 6 * d), 0.5 * d ** -0.5),
        'adaln_b': nrm((DEPTH, 6 * d), 0.01),
        'norm1_g': gain((DEPTH, d)),
        'norm2_g': gain((DEPTH, d)),
        'ffn_w_up': nrm((DEPTH, d, 2 * D_FF), d ** -0.5),
        'ffn_conv_w': nrm((DEPTH, CONV_W, 2 * D_FF), CONV_W ** -0.5),
        'ffn_conv_b': nrm((DEPTH, 2 * D_FF), 0.01),
        'ffn_w_down': nrm((DEPTH, D_FF, d), D_FF ** -0.5),
        'a_w_qkv': nrm((N_LAYERS_A, d, 3 * DIFF_HEADS * 2 * HEAD_DIM), d ** -0.5),
        'a_qk_g': gain((N_LAYERS_A, 2, HEAD_DIM)),
        'a_lambda': nrm((N_LAYERS_A, 4, HEAD_DIM), 0.1),
        'a_head_g': gain((N_LAYERS_A, DIFF_V_DIM)),
        'a_w_o': nrm((N_LAYERS_A, DIFF_HEADS * DIFF_V_DIM, d), (DIFF_HEADS * DIFF_V_DIM) ** -0.5),
        'b_w_qkv': nrm((N_LAYERS_B, d, qkv_gqa), d ** -0.5),
        'b_qk_g': gain((N_LAYERS_B, 2, HEAD_DIM)),
        'b_w_o': nrm((N_LAYERS_B, GQA_HEADS * HEAD_DIM, d), (GQA_HEADS * HEAD_DIM) ** -0.5),
        'c_w_qkv': nrm((N_LAYERS_C, d, qkv_gqa), d ** -0.5),
        'c_qk_g': gain((N_LAYERS_C, 2, HEAD_DIM)),
        'c_sink': nrm((N_LAYERS_C, GQA_HEADS), 0.5),
        'c_w_o': nrm((N_LAYERS_C, GQA_HEADS * HEAD_DIM, d), (GQA_HEADS * HEAD_DIM) ** -0.5),
    }


def reference(x, c, ctx, c_ctx, adaln_w, adaln_b, norm1_g, norm2_g, ffn_w_up, ffn_conv_w, ffn_conv_b,
              ffn_w_down, a_w_qkv, a_qk_g, a_lambda, a_head_g, a_w_o, b_w_qkv, b_qk_g, b_w_o,
              c_w_qkv, c_qk_g, c_sink, c_w_o):
    ROWS = x.shape[1] // GRID_W
    cos, sin = grid_rope_tables(ROWS)
    h_ctx = ctx
    sc = jax.nn.silu(c)
    sc_ctx = jax.nn.silu(c_ctx)
    for i in range(DEPTH):
        last = i == DEPTH - 1
        j = i // N_MIXERS
        kind = i % N_MIXERS
        mod_l = (sc @ adaln_w[i] + adaln_b[i])[:, None, :]
        mod_c = (sc_ctx @ adaln_w[i] + adaln_b[i])[None, None, :]
        sh1, sc1, g1, sh2, sc2, g2 = jnp.split(mod_l, 6, axis=-1)
        csh1, csc1, cg1, csh2, csc2, cg2 = jnp.split(mod_c, 6, axis=-1)
        hn_l = modulate(rms_norm(x, norm1_g[i]), sh1, sc1)
        hn_c = modulate(rms_norm(h_ctx, norm1_g[i]), csh1, csc1)
        if kind == 0:
            lam_init = 0.8 - 0.6 * math.exp(-0.3 * i)
            y_l, y_c = diff_attention(hn_l, hn_c, a_w_qkv[j], a_qk_g[j], a_lambda[j], a_head_g[j], a_w_o[j],
                                      lam_init, cos, sin, not last)
        elif kind == 1:
            y_l, y_c = grid_gqa_attention(hn_l, hn_c, b_w_qkv[j], b_qk_g[j], b_w_o[j], cos, sin, not last)
        else:
            y_l, y_c = window_gqa_attention(hn_l, hn_c, c_w_qkv[j], c_qk_g[j], c_sink[j], c_w_o[j],
                                            cos, sin, not last)
        x = x + g1 * y_l
        x = x + g2 * conv_ffn(modulate(rms_norm(x, norm2_g[i]), sh2, sc2),
                              ffn_w_up[i], ffn_conv_w[i], ffn_conv_b[i], ffn_w_down[i])
        if not last:
            h_ctx = h_ctx + cg1 * y_c
            h_ctx = h_ctx + cg2 * conv_ffn(modulate(rms_norm(h_ctx, norm2_g[i]), csh2, csc2),
                                           ffn_w_up[i], ffn_conv_w[i], ffn_conv_b[i], ffn_w_down[i])
    return x
```

```python
import functools
import math

import jax
import jax.numpy as jnp
from jax import lax
from jax.experimental import pallas as pl
from jax.experimental.pallas import tpu as pltpu

F32 = jnp.float32
BF16 = jnp.bfloat16

D_MODEL = 1024
HEAD_DIM = 64
GRID_W = 64
DIFF_HEADS = D_MODEL // (2 * HEAD_DIM)
DIFF_V_DIM = 2 * HEAD_DIM
GQA_HEADS = D_MODEL // HEAD_DIM
GQA_KV_HEADS = GQA_HEADS // 4
GQA_GROUP = GQA_HEADS // GQA_KV_HEADS
WINDOW = 128
D_FF = 256 * ((8 * D_MODEL // 3 + 255) // 256)
CONV_W = 3
ROPE_THETA = 10000.0
EPS = 1e-6
ATTN_SCALE = HEAD_DIM ** -0.5
N_MIXERS = 3

LANES = 128
BF16_SUBLANES = 16
FF_CHUNK = 256
MOD_ROWS = 8
C_ROWS = 24
VMEM_LIMIT = 56 * 1024 * 1024
NEG_BIG = -1e30


def _cparams(sem, vmem=VMEM_LIMIT):
    return pltpu.CompilerParams(dimension_semantics=sem, vmem_limit_bytes=vmem)


def _resident(shape, index_map):
    return pl.BlockSpec(shape, index_map, pipeline_mode=pl.Buffered(1))


def _adaln_kernel(c_ref, w_ref, b_ref, o_ref):
    c = c_ref[...]
    s = c * jax.nn.sigmoid(c)
    o_ref[0] = jnp.dot(s.astype(BF16), w_ref[0].astype(BF16),
                       preferred_element_type=F32) + b_ref[0]


def _adaln(cvec, adaln_w, adaln_b):
    depth, d, n = adaln_w.shape
    tn = 1536
    return pl.pallas_call(
        _adaln_kernel,
        grid=(depth, n // tn),
        in_specs=[pl.BlockSpec((C_ROWS, d), lambda i, j: (0, 0)),
                  pl.BlockSpec((1, d, tn), lambda i, j: (i, 0, j)),
                  pl.BlockSpec((1, 1, tn), lambda i, j: (i, 0, j))],
        out_specs=pl.BlockSpec((1, C_ROWS, tn), lambda i, j: (i, 0, j)),
        out_shape=jax.ShapeDtypeStruct((depth, C_ROWS, n), F32),
        compiler_params=_cparams(("arbitrary", "arbitrary"), 40 * 1024 * 1024),
        name="adaln",
    )(cvec, adaln_w, adaln_b.reshape(depth, 1, n))


def _rms(x):
    return x * lax.rsqrt(jnp.mean(x * x, axis=-1, keepdims=True) + EPS)


def _head_norm_rope(z, gain, rope, scale):
    lane = lax.broadcasted_iota(jnp.int32, (1, LANES), 1)
    lo = lane < HEAD_DIM
    z2 = z * z
    s_lo = jnp.sum(jnp.where(lo, z2, 0.0), axis=-1, keepdims=True)
    s_hi = jnp.sum(jnp.where(lo, 0.0, z2), axis=-1, keepdims=True)
    msq = jnp.where(lo, s_lo, s_hi) * (1.0 / HEAD_DIM)
    zn = z * lax.rsqrt(msq + EPS) * gain
    if rope is not None:
        cos, sin_a, sin_b = rope
        zn = (zn * cos + pltpu.roll(zn, LANES - HEAD_DIM // 2, 1) * sin_a
              + pltpu.roll(zn, HEAD_DIM // 2, 1) * sin_b)
    if scale != 1.0:
        zn = zn * scale
    return zn


def _qkv_kernel(*refs, diff, rope):
    x_ref, mod_ref, g_ref, w_ref, gq_ref, gk_ref = refs[:6]
    if rope:
        cos_ref, sa_ref, sb_ref = refs[6:9]
        q_ref, k_ref, v_ref = refs[9:]
        tabs = (cos_ref[...], sa_ref[...], sb_ref[...])
    else:
        q_ref, k_ref, v_ref = refs[6:]
        tabs = None
    x = x_ref[0]
    sh = mod_ref[0, 0:1, :]
    sc = mod_ref[0, 1:2, :]
    hn = ((_rms(x) * g_ref[...]) * (1.0 + sc) + sh).astype(BF16)

    nq = D_MODEL
    nk = D_MODEL if diff else GQA_KV_HEADS * HEAD_DIM
    nv = nk
    cw = 2 * LANES

    def section(col0, n, gain_ref, scale, out_ref):
        for c0 in range(0, n, cw):
            z = jnp.dot(hn, w_ref[:, col0 + c0:col0 + c0 + cw], preferred_element_type=F32)
            for s0 in range(0, cw, LANES):
                zn = _head_norm_rope(z[:, s0:s0 + LANES], gain_ref[:, c0 + s0:c0 + s0 + LANES],
                                     tabs, scale).astype(BF16)
                if diff:
                    out_ref[0, :, c0 + s0:c0 + s0 + LANES] = zn
                else:
                    h = (c0 + s0) // HEAD_DIM
                    out_ref[0, h] = zn[:, :HEAD_DIM]
                    out_ref[0, h + 1] = zn[:, HEAD_DIM:]

    section(0, nq, gq_ref, ATTN_SCALE, q_ref)
    section(nq, nk, gk_ref, 1.0, k_ref)
    for c0 in range(0, nv, cw):
        v = jnp.dot(hn, w_ref[:, nq + nk + c0:nq + nk + c0 + cw],
                    preferred_element_type=F32).astype(BF16)
        if diff:
            v_ref[0, :, c0:c0 + cw] = v
        else:
            for s0 in range(0, cw, HEAD_DIM):
                v_ref[0, (c0 + s0) // HEAD_DIM] = v[:, s0:s0 + HEAD_DIM]


def _qkv_proj(x, mod, g, w, gq, gk, tabs, *, diff, tile):
    b, l, d = x.shape
    nmod = mod.shape[0]
    n = w.shape[1]
    rope = tabs is not None
    nt = l // tile
    mod_map = (lambda bi, ti: (bi, 0, 0)) if nmod > 1 else (lambda bi, ti: (0, 0, 0))
    in_specs = [pl.BlockSpec((1, tile, d), lambda bi, ti: (bi, ti, 0)),
                pl.BlockSpec((1, MOD_ROWS, d), mod_map),
                pl.BlockSpec((1, d), lambda bi, ti: (0, 0)),
                _resident((d, n), lambda bi, ti: (0, 0)),
                pl.BlockSpec((1, gq.shape[1]), lambda bi, ti: (0, 0)),
                pl.BlockSpec((1, gk.shape[1]), lambda bi, ti: (0, 0))]
    args = [x, mod, g, w, gq, gk]
    if rope:
        in_specs += [pl.BlockSpec((tile, LANES), lambda bi, ti: (ti, 0))] * 3
        args += list(tabs)
    if diff:
        out_shape = [jax.ShapeDtypeStruct((b, l, D_MODEL), BF16)] * 3
        out_specs = [pl.BlockSpec((1, tile, D_MODEL), lambda bi, ti: (bi, ti, 0))] * 3
    else:
        out_shape = [jax.ShapeDtypeStruct((b, GQA_HEADS, l, HEAD_DIM), BF16),
                     jax.ShapeDtypeStruct((b, GQA_KV_HEADS, l, HEAD_DIM), BF16),
                     jax.ShapeDtypeStruct((b, GQA_KV_HEADS, l, HEAD_DIM), BF16)]
        out_specs = [pl.BlockSpec((1, GQA_HEADS, tile, HEAD_DIM), lambda bi, ti: (bi, 0, ti, 0)),
                     pl.BlockSpec((1, GQA_KV_HEADS, tile, HEAD_DIM), lambda bi, ti: (bi, 0, ti, 0)),
                     pl.BlockSpec((1, GQA_KV_HEADS, tile, HEAD_DIM), lambda bi, ti: (bi, 0, ti, 0))]
    return pl.pallas_call(
        functools.partial(_qkv_kernel, diff=diff, rope=rope),
        grid=(b, nt),
        in_specs=in_specs,
        out_specs=out_specs,
        out_shape=out_shape,
        compiler_params=_cparams(("arbitrary", "arbitrary")),
        name="qkv_diff" if diff else "qkv_gqa",
    )(*args)


def _dot_nt(a, b):
    return lax.dot_general(a, b, (((1,), (1,)), ((), ())), preferred_element_type=F32)


def _diff_attn_kernel(*refs, lam_init, n_src, tq):
    lam_ref, hg_ref, q_ref = refs[:3]
    kv_refs = refs[3:3 + 2 * n_src]
    o_ref = refs[3 + 2 * n_src]
    lq = lam_ref[...]
    lam = (jnp.exp(jnp.sum(lq[0:1] * lq[1:2], axis=-1, keepdims=True))
           - jnp.exp(jnp.sum(lq[2:3] * lq[3:4], axis=-1, keepdims=True)) + lam_init)
    hg = hg_ref[...] * (1.0 - lam_init)
    lane = lax.broadcasted_iota(jnp.int32, (1, LANES), 1)
    lo = lane < HEAD_DIM
    nq = q_ref.shape[1] // tq

    def body(i, carry):
        r0 = pl.multiple_of(i * tq, tq)
        q = q_ref[0, pl.ds(r0, tq), :]
        zero = jnp.zeros_like(q)
        q2 = jnp.concatenate([jnp.where(lo, q, zero), jnp.where(lo, zero, q)], axis=0)
        s = [_dot_nt(q2, kv_refs[2 * j][0]) for j in range(n_src)]
        m = s[0].max(axis=-1, keepdims=True)
        for j in range(1, n_src):
            m = jnp.maximum(m, s[j].max(axis=-1, keepdims=True))
        e = [jnp.exp(sj - m) for sj in s]
        den = e[0].sum(axis=-1, keepdims=True)
        for j in range(1, n_src):
            den = den + e[j].sum(axis=-1, keepdims=True)
        r = 1.0 / den
        c0 = r[:tq]
        c1 = r[tq:] * lam
        o = None
        for j in range(n_src):
            w = (e[j][:tq] * c0 - e[j][tq:] * c1).astype(BF16)
            oj = jnp.dot(w, kv_refs[2 * j + 1][0], preferred_element_type=F32)
            o = oj if o is None else o + oj
        o_ref[0, pl.ds(r0, tq), :] = (_rms(o) * hg).astype(BF16)
        return carry

    lax.fori_loop(0, nq, body, 0)


def _diff_attention(lam_p, head_g, q, kvs, *, lam_init, tq):
    b, lq, d = q.shape
    n_src = len(kvs)
    in_specs = [pl.BlockSpec((4, HEAD_DIM), lambda bi, h: (0, 0)),
                pl.BlockSpec((1, DIFF_V_DIM), lambda bi, h: (0, 0)),
                pl.BlockSpec((1, lq, LANES), lambda bi, h: (bi, 0, h))]
    args = [lam_p, head_g, q]
    for k, v in kvs:
        s = k.shape[1]
        in_specs += [pl.BlockSpec((1, s, LANES), lambda bi, h: (bi, 0, h))] * 2
        args += [k, v]
    return pl.pallas_call(
        functools.partial(_diff_attn_kernel, lam_init=lam_init, n_src=n_src, tq=tq),
        grid=(b, DIFF_HEADS),
        in_specs=in_specs,
        out_specs=pl.BlockSpec((1, lq, LANES), lambda bi, h: (bi, 0, h)),
        out_shape=jax.ShapeDtypeStruct((b, lq, d), BF16),
        compiler_params=_cparams(("arbitrary", "arbitrary")),
        name="diff_attn",
    )(*args)


def _gqa_finish(o_ref, r0, tq, o, r):
    o = o * r
    o_ref[0, pl.ds(r0, tq), :] = jnp.concatenate(
        [o[g * tq:(g + 1) * tq] for g in range(GQA_GROUP)], axis=-1).astype(BF16)


def _sink_rows(sink_ref, kk, tq):
    return jnp.concatenate(
        [jnp.full((tq, 1), sink_ref[kk, g], F32) for g in range(GQA_GROUP)], axis=0)


def _gqa_dense_kernel(*refs, n_src, tq, sink):
    if sink:
        sink_ref = refs[0]
        refs = refs[1:]
    q_ref = refs[0]
    kv_refs = refs[1:1 + 2 * n_src]
    o_ref = refs[1 + 2 * n_src]
    nq = q_ref.shape[2] // tq
    kk = pl.program_id(1)

    def body(i, carry):
        r0 = pl.multiple_of(i * tq, tq)
        q = q_ref[0, :, pl.ds(r0, tq), :].reshape(GQA_GROUP * tq, HEAD_DIM)
        s = [_dot_nt(q, kv_refs[2 * j][0, 0]) for j in range(n_src)]
        m = s[0].max(axis=-1, keepdims=True)
        for j in range(1, n_src):
            m = jnp.maximum(m, s[j].max(axis=-1, keepdims=True))
        if sink:
            sv = _sink_rows(sink_ref, kk, tq)
            m = jnp.maximum(m, sv)
        e = [jnp.exp(sj - m) for sj in s]
        den = e[0].sum(axis=-1, keepdims=True)
        for j in range(1, n_src):
            den = den + e[j].sum(axis=-1, keepdims=True)
        if sink:
            den = den + jnp.exp(sv - m)
        o = None
        for j in range(n_src):
            oj = jnp.dot(e[j].astype(BF16), kv_refs[2 * j + 1][0, 0], preferred_element_type=F32)
            o = oj if o is None else o + oj
        _gqa_finish(o_ref, r0, tq, o, 1.0 / den)
        return carry

    lax.fori_loop(0, nq, body, 0)


def _gqa_dense_attention(q, kvs, *, tq, sink=None):
    b, _, lq, _ = q.shape
    n_src = len(kvs)
    in_specs = []
    args = []
    if sink is not None:
        in_specs.append(pl.BlockSpec(memory_space=pltpu.SMEM))
        args.append(sink)
    in_specs.append(pl.BlockSpec((1, GQA_GROUP, lq, HEAD_DIM), lambda bi, kk: (bi, kk, 0, 0)))
    args.append(q)
    for k, v in kvs:
        s = k.shape[2]
        in_specs += [pl.BlockSpec((1, 1, s, HEAD_DIM), lambda bi, kk: (bi, kk, 0, 0))] * 2
        args += [k, v]
    return pl.pallas_call(
        functools.partial(_gqa_dense_kernel, n_src=n_src, tq=tq, sink=sink is not None),
        grid=(b, GQA_KV_HEADS),
        in_specs=in_specs,
        out_specs=pl.BlockSpec((1, lq, GQA_GROUP * HEAD_DIM), lambda bi, kk: (bi, 0, kk)),
        out_shape=jax.ShapeDtypeStruct((b, lq, D_MODEL), BF16),
        compiler_params=_cparams(("arbitrary", "arbitrary")),
        name="gqa_dense_attn",
    )(*args)


def _gqa_window_kernel(sink_ref, q_ref, kl_ref, vl_ref, kc_ref, vc_ref, o_ref, *, tq):
    n_lat = q_ref.shape[2]
    nq = n_lat // tq
    band = min(tq + 2 * WINDOW, n_lat)
    kk = pl.program_id(1)
    rows = GQA_GROUP * tq
    qoff = lax.broadcasted_iota(jnp.int32, (tq, band), 0)
    koff = lax.broadcasted_iota(jnp.int32, (tq, band), 1)
    rel0 = jnp.concatenate([koff - qoff] * GQA_GROUP, axis=0)
    sv = _sink_rows(sink_ref, kk, tq)

    def body(i, carry):
        r0 = pl.multiple_of(i * tq, tq)
        start = pl.multiple_of(jnp.clip(r0 - WINDOW, 0, n_lat - band), WINDOW)
        q = q_ref[0, :, pl.ds(r0, tq), :].reshape(rows, HEAD_DIM)
        kb = kl_ref[0, 0, pl.ds(start, band), :]
        vb = vl_ref[0, 0, pl.ds(start, band), :]
        rel = rel0 + (start - r0)
        s_w = jnp.where(jnp.abs(rel) <= WINDOW, _dot_nt(q, kb), NEG_BIG)
        s_c = _dot_nt(q, kc_ref[0, 0])
        m = jnp.maximum(jnp.maximum(s_w.max(axis=-1, keepdims=True),
                                    s_c.max(axis=-1, keepdims=True)), sv)
        e_w = jnp.exp(s_w - m)
        e_c = jnp.exp(s_c - m)
        den = (e_w.sum(axis=-1, keepdims=True) + e_c.sum(axis=-1, keepdims=True)
               + jnp.exp(sv - m))
        o = (jnp.dot(e_w.astype(BF16), vb, preferred_element_type=F32)
             + jnp.dot(e_c.astype(BF16), vc_ref[0, 0], preferred_element_type=F32))
        _gqa_finish(o_ref, r0, tq, o, 1.0 / den)
        return carry

    lax.fori_loop(0, nq, body, 0)


def _gqa_window_attention(q, k_l, v_l, k_c, v_c, sink, *, tq):
    b, _, lq, _ = q.shape
    lc = k_c.shape[2]
    return pl.pallas_call(
        functools.partial(_gqa_window_kernel, tq=tq),
        grid=(b, GQA_KV_HEADS),
        in_specs=[pl.BlockSpec(memory_space=pltpu.SMEM),
                  pl.BlockSpec((1, GQA_GROUP, lq, HEAD_DIM), lambda bi, kk: (bi, kk, 0, 0)),
                  pl.BlockSpec((1, 1, lq, HEAD_DIM), lambda bi, kk: (bi, kk, 0, 0)),
                  pl.BlockSpec((1, 1, lq, HEAD_DIM), lambda bi, kk: (bi, kk, 0, 0)),
                  pl.BlockSpec((1, 1, lc, HEAD_DIM), lambda bi, kk: (bi, kk, 0, 0)),
                  pl.BlockSpec((1, 1, lc, HEAD_DIM), lambda bi, kk: (bi, kk, 0, 0))],
        out_specs=pl.BlockSpec((1, lq, GQA_GROUP * HEAD_DIM), lambda bi, kk: (bi, 0, kk)),
        out_shape=jax.ShapeDtypeStruct((b, lq, D_MODEL), BF16),
        compiler_params=_cparams(("arbitrary", "arbitrary")),
        name="gqa_window_attn",
    )(sink, q, k_l, v_l, k_c, v_c)


def _oproj_kernel(o_ref, w_ref, x_ref, mod_ref, g_ref, xo_ref, hn_ref):
    y = jnp.dot(o_ref[0], w_ref[...], preferred_element_type=F32)
    g1 = mod_ref[0, 2:3, :]
    sh2 = mod_ref[0, 3:4, :]
    sc2 = mod_ref[0, 4:5, :]
    xn = x_ref[0] + g1 * y
    xo_ref[0] = xn
    hn_ref[0] = ((_rms(xn) * g_ref[...]) * (1.0 + sc2) + sh2).astype(BF16)


def _oproj(o, w, x, mod, g, *, tile):
    b, l, d = x.shape
    nmod = mod.shape[0]
    mod_map = (lambda bi, ti: (bi, 0, 0)) if nmod > 1 else (lambda bi, ti: (0, 0, 0))
    row = lambda bi, ti: (bi, ti, 0)
    return pl.pallas_call(
        _oproj_kernel,
        grid=(b, l // tile),
        in_specs=[pl.BlockSpec((1, tile, d), row),
                  _resident((d, d), lambda bi, ti: (0, 0)),
                  pl.BlockSpec((1, tile, d), row),
                  pl.BlockSpec((1, MOD_ROWS, d), mod_map),
                  pl.BlockSpec((1, d), lambda bi, ti: (0, 0))],
        out_specs=[pl.BlockSpec((1, tile, d), row), pl.BlockSpec((1, tile, d), row)],
        out_shape=[jax.ShapeDtypeStruct((b, l, d), F32), jax.ShapeDtypeStruct((b, l, d), BF16)],
        input_output_aliases={2: 0},
        compiler_params=_cparams(("arbitrary", "arbitrary")),
        name="oproj",
    )(o, w, x, mod, g)


def _ffn_kernel(hp_ref, h_ref, hx_ref, x_ref, mod_ref, wu_ref, cw_ref, cb_ref, wd_ref,
                o_ref, lhs_ref, *, tile, tiles_per_seq):
    ti = pl.program_id(1)
    halo = BF16_SUBLANES
    first = ti == 0
    last = ti == tiles_per_seq - 1
    lhs_ref[0:halo, :] = jnp.where(first, jnp.zeros_like(hp_ref[0]), hp_ref[0])
    lhs_ref[halo:halo + tile, :] = h_ref[0]
    lhs_ref[halo + tile:, :] = jnp.where(last, jnp.zeros_like(hx_ref[0]), hx_ref[0])
    lhs = lhs_ref[...]
    m = tile + 2 * halo

    def conv(u, c0):
        prev = pltpu.roll(u, 1, 0)[halo:halo + tile]
        nxt = pltpu.roll(u, m - 1, 0)[halo:halo + tile]
        cur = u[halo:halo + tile]
        return (prev * cw_ref[0:1, c0:c0 + FF_CHUNK] + cur * cw_ref[1:2, c0:c0 + FF_CHUNK]
                + nxt * cw_ref[2:3, c0:c0 + FF_CHUNK] + cb_ref[:, c0:c0 + FF_CHUNK])

    acc = None
    for j in range(D_FF // FF_CHUNK):
        cg = j * FF_CHUNK
        cv = D_FF + j * FF_CHUNK
        gate = conv(jnp.dot(lhs, wu_ref[:, cg:cg + FF_CHUNK], preferred_element_type=F32), cg)
        val = conv(jnp.dot(lhs, wu_ref[:, cv:cv + FF_CHUNK], preferred_element_type=F32), cv)
        a = (gate * jax.nn.sigmoid(gate) * val).astype(BF16)
        part = jnp.dot(a, wd_ref[cg:cg + FF_CHUNK, :], preferred_element_type=F32)
        acc = part if acc is None else acc + part
    g2 = mod_ref[0, 5:6, :]
    o_ref[0] = x_ref[0] + g2 * acc


def _ffn(hn, x, mod, w_up, conv_w, conv_b, w_down, *, tile):
    b, l, d = x.shape
    nmod = mod.shape[0]
    nt = l // tile
    hb = tile // BF16_SUBLANES
    nhb = l // BF16_SUBLANES
    mod_map = (lambda bi, ti: (bi, 0, 0)) if nmod > 1 else (lambda bi, ti: (0, 0, 0))
    row = lambda bi, ti: (bi, ti, 0)
    const2 = lambda bi, ti: (0, 0)
    return pl.pallas_call(
        functools.partial(_ffn_kernel, tile=tile, tiles_per_seq=nt),
        grid=(b, nt),
        in_specs=[pl.BlockSpec((1, BF16_SUBLANES, d),
                               lambda bi, ti: (bi, jnp.maximum(ti * hb - 1, 0), 0)),
                  pl.BlockSpec((1, tile, d), row),
                  pl.BlockSpec((1, BF16_SUBLANES, d),
                               lambda bi, ti: (bi, jnp.minimum((ti + 1) * hb, nhb - 1), 0)),
                  pl.BlockSpec((1, tile, d), row),
                  pl.BlockSpec((1, MOD_ROWS, d), mod_map),
                  _resident((d, 2 * D_FF), const2),
                  pl.BlockSpec((CONV_W, 2 * D_FF), const2),
                  pl.BlockSpec((1, 2 * D_FF), const2),
                  _resident((D_FF, d), const2)],
        out_specs=pl.BlockSpec((1, tile, d), row),
        out_shape=jax.ShapeDtypeStruct((b, l, d), F32),
        scratch_shapes=[pltpu.VMEM((tile + 2 * BF16_SUBLANES, d), BF16)],
        input_output_aliases={3: 0},
        compiler_params=_cparams(("arbitrary", "arbitrary")),
        name="conv_ffn",
    )(hn, hn, hn, x, mod, w_up, conv_w, conv_b, w_down)


def _rope_tables(n_lat):
    t = jnp.arange(n_lat)
    row = (t // GRID_W).astype(F32)
    col = (t % GRID_W).astype(F32)
    n_freq = HEAD_DIM // 4
    inv_freq = ROPE_THETA ** (-jnp.arange(n_freq, dtype=F32) / n_freq)
    ang = jnp.concatenate([row[:, None] * inv_freq, col[:, None] * inv_freq], axis=-1)
    cos = jnp.cos(ang)
    sin = jnp.sin(ang)
    zero = jnp.zeros_like(sin)
    cos_h = jnp.concatenate([cos, cos], axis=-1)
    sa_h = jnp.concatenate([-sin, zero], axis=-1)
    sb_h = jnp.concatenate([zero, sin], axis=-1)
    rep = LANES // HEAD_DIM
    return tuple(jnp.tile(t, (1, rep)) for t in (cos_h, sa_h, sb_h))


def _tile_gain(g, n):
    return jnp.tile(g, n // g.shape[0]).reshape(1, n)


def kernel(x, c, ctx, c_ctx, adaln_w, adaln_b, norm1_g, norm2_g, ffn_w_up, ffn_conv_w, ffn_conv_b, ffn_w_down, a_w_qkv, a_qk_g, a_lambda, a_head_g, a_w_o, b_w_qkv, b_qk_g, b_w_o, c_w_qkv, c_qk_g, c_sink, c_w_o):
    b, n_lat, d = x.shape
    n_ctx = ctx.shape[1]
    depth = adaln_w.shape[0]
    assert d == D_MODEL and b + 1 <= C_ROWS

    tabs = _rope_tables(n_lat)

    cvec = jnp.concatenate([c, c_ctx[None, :], jnp.zeros((C_ROWS - b - 1, d), F32)], axis=0)
    mod_all = _adaln(cvec, adaln_w, adaln_b).reshape(depth, C_ROWS, 6, d)
    mod_all = jnp.pad(mod_all, ((0, 0), (0, 0), (0, MOD_ROWS - 6), (0, 0)))

    lat_tile = min(512, n_lat)
    ctx_tile = n_ctx
    tq_lat = min(256, n_lat)
    tq_ctx = min(256, n_ctx)
    tq_win = min(128, n_lat)

    h_ctx = ctx
    for i in range(depth):
        last = i == depth - 1
        j = i // N_MIXERS
        kind = i % N_MIXERS
        mod_l = mod_all[i, :b]
        mod_c = mod_all[i, b:b + 1]
        g1n = norm1_g[i].reshape(1, d)
        g2n = norm2_g[i].reshape(1, d)
        if kind == 0:
            w_qkv, qk_g, w_o = a_w_qkv[j], a_qk_g[j], a_w_o[j]
        elif kind == 1:
            w_qkv, qk_g, w_o = b_w_qkv[j], b_qk_g[j], b_w_o[j]
        else:
            w_qkv, qk_g, w_o = c_w_qkv[j], c_qk_g[j], c_w_o[j]
        diff = kind == 0
        w_qkv = w_qkv.astype(BF16)
        w_o = w_o.astype(BF16)
        nk = D_MODEL if diff else GQA_KV_HEADS * HEAD_DIM
        gq = _tile_gain(qk_g[0], D_MODEL)
        gk = _tile_gain(qk_g[1], nk)

        q_l, k_l, v_l = _qkv_proj(x, mod_l, g1n, w_qkv, gq, gk, tabs, diff=diff, tile=lat_tile)
        q_c, k_c, v_c = _qkv_proj(h_ctx, mod_c, g1n, w_qkv, gq, gk, None, diff=diff, tile=ctx_tile)

        o_c = None
        if kind == 0:
            lam_init = 0.8 - 0.6 * math.exp(-0.3 * i)
            hg = a_head_g[j].reshape(1, DIFF_V_DIM)
            o_l = _diff_attention(a_lambda[j], hg, q_l, [(k_l, v_l), (k_c, v_c)],
                                  lam_init=lam_init, tq=tq_lat)
            if not last:
                o_c = _diff_attention(a_lambda[j], hg, q_c, [(k_c, v_c)],
                                      lam_init=lam_init, tq=tq_ctx)
        elif kind == 1:
            o_l = _gqa_dense_attention(q_l, [(k_l, v_l), (k_c, v_c)], tq=tq_lat)
            if not last:
                o_c = _gqa_dense_attention(q_c, [(k_c, v_c)], tq=tq_ctx)
        else:
            sink = c_sink[j].reshape(GQA_KV_HEADS, GQA_GROUP)
            o_l = _gqa_window_attention(q_l, k_l, v_l, k_c, v_c, sink, tq=tq_win)
            if not last:
                o_c = _gqa_dense_attention(q_c, [(k_c, v_c)], tq=tq_ctx, sink=sink)

        w_up = ffn_w_up[i].astype(BF16)
        w_down = ffn_w_down[i].astype(BF16)
        conv_b = ffn_conv_b[i].reshape(1, 2 * D_FF)
        x, hn2 = _oproj(o_l, w_o, x, mod_l, g2n, tile=lat_tile)
        x = _ffn(hn2, x, mod_l, w_up, ffn_conv_w[i], conv_b, w_down, tile=lat_tile)
        if not last:
            h_ctx, hn2c = _oproj(o_c, w_o, h_ctx, mod_c, g2n, tile=ctx_tile)
            h_ctx = _ffn(hn2c, h_ctx, mod_c, w_up, ffn_conv_w[i], conv_b, w_down, tile=ctx_tile)
    return x
```

```python
import functools
import math

import numpy as np
import jax
import jax.numpy as jnp
from jax import lax
from jax.experimental import pallas as pl
from jax.experimental.pallas import tpu as pltpu

F32 = jnp.float32
BF16 = jnp.bfloat16

D_MODEL = 1024
HEAD_DIM = 64
GRID_W = 64
DIFF_HEADS = D_MODEL // (2 * HEAD_DIM)
DIFF_V_DIM = 2 * HEAD_DIM
GQA_HEADS = D_MODEL // HEAD_DIM
GQA_KV_HEADS = GQA_HEADS // 4
GQA_GROUP = GQA_HEADS // GQA_KV_HEADS
WINDOW = 128
D_FF = 256 * ((8 * D_MODEL // 3 + 255) // 256)
CONV_W = 3
ROPE_THETA = 10000.0
EPS = 1e-6
ATTN_SCALE = HEAD_DIM ** -0.5
LOG2E = 1.0 / math.log(2.0)
Q_SCALE = ATTN_SCALE * LOG2E
N_MIXERS = 3

LANES = 128
QUARTER = HEAD_DIM // 2
BF16_SUBLANES = 16
MXU_W = 256
FF_CHUNK = MXU_W
MOD_ROWS = 8
C_ROWS = 24
VMEM_LIMIT = 56 * 1024 * 1024
NEG_BIG = -1e30


def _cparams(sem, vmem=VMEM_LIMIT):
    return pltpu.CompilerParams(dimension_semantics=sem, vmem_limit_bytes=vmem)


def _resident(shape, index_map):
    return pl.BlockSpec(shape, index_map, pipeline_mode=pl.Buffered(1))


def _rms(x):
    return x * lax.rsqrt(jnp.mean(x * x, axis=-1, keepdims=True) + EPS)


def _head_a_mask():
    lane = lax.broadcasted_iota(jnp.int32, (1, LANES), 1)
    return (lane & QUARTER) == 0


def _adaln_kernel(c_ref, w_ref, b_ref, o_ref):
    c = c_ref[...]
    s = c * jax.nn.sigmoid(c)
    o_ref[0] = jnp.dot(s.astype(BF16), w_ref[0].astype(BF16),
                       preferred_element_type=F32) + b_ref[0]


def _adaln(cvec, adaln_w, adaln_b):
    depth, d, n = adaln_w.shape
    tn = 1536
    return pl.pallas_call(
        _adaln_kernel,
        grid=(depth, n // tn),
        in_specs=[pl.BlockSpec((C_ROWS, d), lambda i, j: (0, 0)),
                  pl.BlockSpec((1, d, tn), lambda i, j: (i, 0, j)),
                  pl.BlockSpec((1, 1, tn), lambda i, j: (i, 0, j))],
        out_specs=pl.BlockSpec((1, C_ROWS, tn), lambda i, j: (i, 0, j)),
        out_shape=jax.ShapeDtypeStruct((depth, C_ROWS, n), F32),
        compiler_params=_cparams(("arbitrary", "arbitrary"), 40 * 1024 * 1024),
        name="adaln",
    )(cvec, adaln_w, adaln_b.reshape(depth, 1, n))


def _qkv_kernel(*refs, rope, n_q, n_norm, v_add):
    x_ref, mod_ref, g_ref, w_ref, gain_ref, bd_ref = refs[:6]
    pos = 6
    if rope:
        cos = refs[pos][...]
        sin = refs[pos + 1][...]
        pos += 2
    if v_add:
        va_ref = refs[pos]
        pos += 1
    q_ref, k_ref, v_ref = refs[pos:]
    n_total = w_ref.shape[1]

    x = x_ref[0]
    sh = mod_ref[0, 0:1, :]
    sc = mod_ref[0, 1:2, :]
    hn = ((_rms(x) * g_ref[...]) * (1.0 + sc) + sh).astype(BF16)

    def project(c0):
        return jnp.dot(hn, w_ref[:, c0:c0 + MXU_W], preferred_element_type=F32)

    def mean_sq(z):
        return jnp.dot((z * z).astype(BF16), bd_ref[...], preferred_element_type=F32)

    def finish(c0, z, msq):
        if c0 < n_norm:
            z = z * lax.rsqrt(msq + EPS) * gain_ref[:, c0:c0 + MXU_W]
            if rope:
                z = jnp.concatenate(
                    [z[:, s0:s0 + LANES] * cos + pltpu.roll(z[:, s0:s0 + LANES], HEAD_DIM, 1) * sin
                     for s0 in range(0, MXU_W, LANES)], axis=-1)
            if c0 < n_q:
                q_ref[0, :, c0:c0 + MXU_W] = z.astype(BF16)
            else:
                k_ref[0, :, c0 - n_q:c0 - n_q + MXU_W] = z.astype(BF16)
        else:
            if v_add:
                z = z + va_ref[:, c0 - n_norm:c0 - n_norm + MXU_W]
            v_ref[0, :, c0 - n_norm:c0 - n_norm + MXU_W] = z.astype(BF16)

    starts = list(range(0, n_total, MXU_W))
    zs, ms = {}, {}
    for step in range(len(starts) + 2):
        if step < len(starts):
            zs[step] = project(starts[step])
        if 0 <= step - 1 < len(starts) and starts[step - 1] < n_norm:
            ms[step - 1] = mean_sq(zs[step - 1])
        if 0 <= step - 2 < len(starts):
            finish(starts[step - 2], zs.pop(step - 2), ms.pop(step - 2, None))


def _qkv_proj(x, mod, g, w, gain, bd, tabs, v_add, *, n_q, n_norm, tile):
    b, l, d = x.shape
    nmod = mod.shape[0]
    n = w.shape[1]
    n_k = n_norm - n_q
    n_v = n - n_norm
    rope = tabs is not None
    mod_map = (lambda bi, ti: (bi, 0, 0)) if nmod > 1 else (lambda bi, ti: (0, 0, 0))
    const2 = lambda bi, ti: (0, 0)
    row = lambda bi, ti: (bi, ti, 0)
    in_specs = [pl.BlockSpec((1, tile, d), row),
                pl.BlockSpec((1, MOD_ROWS, d), mod_map),
                pl.BlockSpec((1, d), const2),
                _resident((d, n), const2),
                pl.BlockSpec((1, n_norm), const2),
                pl.BlockSpec((MXU_W, MXU_W), const2)]
    args = [x, mod, g, w, gain, bd]
    if rope:
        in_specs += [pl.BlockSpec((tile, LANES), lambda bi, ti: (ti, 0))] * 2
        args += list(tabs)
    if v_add is not None:
        in_specs.append(pl.BlockSpec((1, n_v), const2))
        args.append(v_add)
    return pl.pallas_call(
        functools.partial(_qkv_kernel, rope=rope, n_q=n_q, n_norm=n_norm, v_add=v_add is not None),
        grid=(b, l // tile),
        in_specs=in_specs,
        out_specs=[pl.BlockSpec((1, tile, n_q), row), pl.BlockSpec((1, tile, n_k), row),
                   pl.BlockSpec((1, tile, n_v), row)],
        out_shape=[jax.ShapeDtypeStruct((b, l, n_q), BF16), jax.ShapeDtypeStruct((b, l, n_k), BF16),
                   jax.ShapeDtypeStruct((b, l, n_v), BF16)],
        compiler_params=_cparams(("arbitrary", "arbitrary")),
        name="qkv_proj",
    )(*args)


def _dot_nt(a, b):
    return lax.dot_general(a, b, (((1,), (1,)), ((), ())), preferred_element_type=F32)


def _row_max(s_list):
    m = s_list[0].max(axis=-1, keepdims=True)
    for s in s_list[1:]:
        m = jnp.maximum(m, s.max(axis=-1, keepdims=True))
    return m


def _split_heads(qp):
    mask_a = _head_a_mask()
    zero = jnp.zeros_like(qp)
    return jnp.where(mask_a, qp, zero), jnp.where(mask_a, zero, qp)


def _diff_attn_kernel(*refs, lam_init, n_src, tq):
    lam_ref, hg_ref, q_ref = refs[:3]
    kv_refs = refs[3:3 + 2 * n_src]
    o_ref = refs[3 + 2 * n_src]
    s_ref = refs[4 + 2 * n_src]
    w_ref = refs[5 + 2 * n_src]
    lq = lam_ref[...]
    lam = (jnp.exp(jnp.sum(lq[0:1] * lq[1:2], axis=-1, keepdims=True))
           - jnp.exp(jnp.sum(lq[2:3] * lq[3:4], axis=-1, keepdims=True)) + lam_init)
    hg = hg_ref[...] * (1.0 - lam_init)
    n_tiles = q_ref.shape[1] // tq
    sizes = [kv_refs[2 * j].shape[1] for j in range(n_src)]
    offs = [sum(sizes[:j]) for j in range(n_src)]

    rc = min(16, tq)
    n_rc = tq // rc

    def score_parts(t, slot):
        qa, qb = _split_heads(q_ref[0, pl.ds(pl.multiple_of(t * tq, tq), tq), :])
        parts = []
        for j in reversed(range(n_src)):
            for comp, qh in ((0, qa), (1, qb)):
                def part(j=j, comp=comp, qh=qh):
                    s_ref[slot, comp, :, offs[j]:offs[j] + sizes[j]] = _dot_nt(qh, kv_refs[2 * j][0])
                parts.append(part)
        return parts

    def softmax_rows(slot, r):
        rows = slice(r * rc, (r + 1) * rc)
        sa = s_ref[slot, 0, rows, :]
        sb = s_ref[slot, 1, rows, :]
        ea = jnp.exp2(sa - sa.max(axis=-1, keepdims=True))
        eb = jnp.exp2(sb - sb.max(axis=-1, keepdims=True))
        la = ea.sum(axis=-1, keepdims=True)
        lb = eb.sum(axis=-1, keepdims=True)
        w_ref[rows, :] = (ea - eb * (lam * la / lb)).astype(BF16)
        return la

    def values(t, la):
        o = None
        for j in range(n_src):
            oj = jnp.dot(w_ref[:, offs[j]:offs[j] + sizes[j]], kv_refs[2 * j + 1][0],
                         preferred_element_type=F32)
            o = oj if o is None else o + oj
        o = o * (1.0 / la)
        o_ref[0, pl.ds(pl.multiple_of(t * tq, tq), tq), :] = (_rms(o) * hg).astype(BF16)

    def step(t_next, slot_next, t_cur, slot_cur):
        parts = score_parts(t_next, slot_next) if t_next is not None else []
        every = max(1, n_rc // max(1, len(parts)))
        las = []
        for r in range(n_rc):
            if r % every == 0 and parts:
                parts.pop(0)()
            las.append(softmax_rows(slot_cur, r))
        for part in parts:
            part()
        values(t_cur, jnp.concatenate(las, axis=0))

    assert n_tiles % 2 == 0
    for part in score_parts(0, 0):
        part()

    def body(p, carry):
        t = 2 * p
        step(t + 1, 1, t, 0)
        step(t + 2, 0, t + 1, 1)
        return carry

    lax.fori_loop(0, n_tiles // 2 - 1, body, 0)
    step(n_tiles - 1, 1, n_tiles - 2, 0)
    step(None, None, n_tiles - 1, 1)


def _diff_attention(lam_p, head_g, q, kvs, *, lam_init, tq):
    b, lq, d = q.shape
    n_src = len(kvs)
    s_tot = sum(k.shape[1] for k, _ in kvs)
    blk = lambda bi, h: (bi, 0, h)
    in_specs = [pl.BlockSpec((4, HEAD_DIM), lambda bi, h: (0, 0)),
                pl.BlockSpec((1, DIFF_V_DIM), lambda bi, h: (0, 0)),
                pl.BlockSpec((1, lq, LANES), blk)]
    args = [lam_p, head_g, q]
    for k, v in kvs:
        in_specs += [pl.BlockSpec((1, k.shape[1], LANES), blk)] * 2
        args += [k, v]
    return pl.pallas_call(
        functools.partial(_diff_attn_kernel, lam_init=lam_init, n_src=n_src, tq=tq),
        grid=(b, DIFF_HEADS),
        in_specs=in_specs,
        out_specs=pl.BlockSpec((1, lq, LANES), blk),
        out_shape=jax.ShapeDtypeStruct((b, lq, d), BF16),
        scratch_shapes=[pltpu.VMEM((2, 2, tq, s_tot), F32), pltpu.VMEM((tq, s_tot), BF16)],
        compiler_params=_cparams(("arbitrary", "arbitrary")),
        name="diff_attn",
    )(*args)


def _gqa_head(s_list, v_list, sink_logit):
    m = _row_max(s_list)
    if sink_logit is not None:
        m = jnp.maximum(m, sink_logit)
    o = None
    for s, v in zip(s_list, v_list):
        oj = jnp.dot(jnp.exp2(s - m).astype(BF16), v, preferred_element_type=F32)
        o = oj if o is None else o + oj
    den = o[:, HEAD_DIM:HEAD_DIM + 1]
    if sink_logit is not None:
        den = den + jnp.exp2(sink_logit - m)
    return o[:, :HEAD_DIM] * (1.0 / den)


def _group_heads(q_ref, r0, tq):
    heads = []
    for p0 in range(0, GQA_GROUP * HEAD_DIM, LANES):
        heads += list(_split_heads(q_ref[0, pl.ds(r0, tq), p0:p0 + LANES]))
    return heads


def _gqa_tiles(q_ref, o_ref, tq, tile_ctx, logits, finish):
    nq = q_ref.shape[1] // tq
    per_trip = 2 if nq % 2 == 0 else 1

    def body(i, carry):
        pending = None
        outs = []
        for u in range(per_trip):
            r0 = pl.multiple_of((i * per_trip + u) * tq, tq)
            ctx = tile_ctx(r0)
            for g, q in enumerate(_group_heads(q_ref, r0, tq)):
                s = logits(ctx, q)
                if pending is not None:
                    outs.append(finish(*pending))
                pending = (ctx, s, g)
                if g == 0 and u > 0:
                    r_prev = pl.multiple_of((i * per_trip + u - 1) * tq, tq)
                    o_ref[0, pl.ds(r_prev, tq), :] = jnp.concatenate(outs, axis=-1).astype(BF16)
                    outs = []
        outs.append(finish(*pending))
        o_ref[0, pl.ds(r0, tq), :] = jnp.concatenate(outs, axis=-1).astype(BF16)
        return carry

    lax.fori_loop(0, nq // per_trip, body, 0)


def _gqa_dense_kernel(*refs, n_src, tq, sink):
    if sink:
        sink_ref = refs[0]
        refs = refs[1:]
    q_ref = refs[0]
    kv_refs = refs[1:1 + 2 * n_src]
    o_ref = refs[1 + 2 * n_src]
    kk = pl.program_id(1)

    def logits(ctx, q):
        return [_dot_nt(q, kv_refs[2 * j][0]) for j in range(n_src)]

    def finish(ctx, s, g):
        v = [kv_refs[2 * j + 1][0] for j in range(n_src)]
        return _gqa_head(s, v, sink_ref[kk, g] * LOG2E if sink else None)

    _gqa_tiles(q_ref, o_ref, tq, lambda r0: None, logits, finish)


def _gqa_dense_attention(q, kvs, *, tq, sink=None):
    b, lq, d = q.shape
    n_src = len(kvs)
    in_specs = []
    args = []
    if sink is not None:
        in_specs.append(pl.BlockSpec(memory_space=pltpu.SMEM))
        args.append(sink)
    blk = lambda bi, kk: (bi, 0, kk)
    in_specs.append(pl.BlockSpec((1, lq, GQA_GROUP * HEAD_DIM), blk))
    args.append(q)
    for k, v in kvs:
        in_specs += [pl.BlockSpec((1, k.shape[1], LANES), blk)] * 2
        args += [k, v]
    return pl.pallas_call(
        functools.partial(_gqa_dense_kernel, n_src=n_src, tq=tq, sink=sink is not None),
        grid=(b, GQA_KV_HEADS),
        in_specs=in_specs,
        out_specs=pl.BlockSpec((1, lq, GQA_GROUP * HEAD_DIM), blk),
        out_shape=jax.ShapeDtypeStruct((b, lq, d), BF16),
        compiler_params=_cparams(("arbitrary", "arbitrary")),
        name="gqa_dense_attn",
    )(*args)


def _gqa_window_kernel(sink_ref, q_ref, kl_ref, vl_ref, kc_ref, vc_ref, o_ref, *, tq):
    n_lat = q_ref.shape[1]
    band = min(tq + 2 * WINDOW, n_lat)
    kk = pl.program_id(1)
    rel0 = (lax.broadcasted_iota(jnp.int32, (tq, band), 1)
            - lax.broadcasted_iota(jnp.int32, (tq, band), 0))

    def tile_ctx(r0):
        start = pl.multiple_of(jnp.clip(r0 - WINDOW, 0, n_lat - band), WINDOW)
        kb = kl_ref[0, pl.ds(start, band), :]
        vb = vl_ref[0, pl.ds(start, band), :]
        in_band = jnp.abs(rel0 + (start - r0)) <= WINDOW
        return kb, vb, in_band

    def logits(ctx, q):
        kb, _, in_band = ctx
        return [jnp.where(in_band, _dot_nt(q, kb), NEG_BIG), _dot_nt(q, kc_ref[0])]

    def finish(ctx, s, g):
        return _gqa_head(s, [ctx[1], vc_ref[0]], sink_ref[kk, g] * LOG2E)

    _gqa_tiles(q_ref, o_ref, tq, tile_ctx, logits, finish)


def _gqa_window_attention(q, k_l, v_l, k_c, v_c, sink, *, tq):
    b, lq, d = q.shape
    lc = k_c.shape[1]
    blk = lambda bi, kk: (bi, 0, kk)
    return pl.pallas_call(
        functools.partial(_gqa_window_kernel, tq=tq),
        grid=(b, GQA_KV_HEADS),
        in_specs=[pl.BlockSpec(memory_space=pltpu.SMEM),
                  pl.BlockSpec((1, lq, GQA_GROUP * HEAD_DIM), blk),
                  pl.BlockSpec((1, lq, LANES), blk),
                  pl.BlockSpec((1, lq, LANES), blk),
                  pl.BlockSpec((1, lc, LANES), blk),
                  pl.BlockSpec((1, lc, LANES), blk)],
        out_specs=pl.BlockSpec((1, lq, GQA_GROUP * HEAD_DIM), blk),
        out_shape=jax.ShapeDtypeStruct((b, lq, d), BF16),
        compiler_params=_cparams(("arbitrary", "arbitrary")),
        name="gqa_window_attn",
    )(sink, q, k_l, v_l, k_c, v_c)


def _oproj_kernel(o_ref, w_ref, x_ref, mod_ref, g_ref, xo_ref, hn_ref):
    y = jnp.dot(o_ref[0], w_ref[...], preferred_element_type=F32)
    g1 = mod_ref[0, 2:3, :]
    sh2 = mod_ref[0, 3:4, :]
    sc2 = mod_ref[0, 4:5, :]
    xn = x_ref[0] + g1 * y
    xo_ref[0] = xn
    hn_ref[0] = ((_rms(xn) * g_ref[...]) * (1.0 + sc2) + sh2).astype(BF16)


def _oproj(o, w, x, mod, g, *, tile):
    b, l, d = x.shape
    nmod = mod.shape[0]
    mod_map = (lambda bi, ti: (bi, 0, 0)) if nmod > 1 else (lambda bi, ti: (0, 0, 0))
    row = lambda bi, ti: (bi, ti, 0)
    return pl.pallas_call(
        _oproj_kernel,
        grid=(b, l // tile),
        in_specs=[pl.BlockSpec((1, tile, d), row),
                  _resident((d, d), lambda bi, ti: (0, 0)),
                  pl.BlockSpec((1, tile, d), row),
                  pl.BlockSpec((1, MOD_ROWS, d), mod_map),
                  pl.BlockSpec((1, d), lambda bi, ti: (0, 0))],
        out_specs=[pl.BlockSpec((1, tile, d), row), pl.BlockSpec((1, tile, d), row)],
        out_shape=[jax.ShapeDtypeStruct((b, l, d), F32), jax.ShapeDtypeStruct((b, l, d), BF16)],
        input_output_aliases={2: 0},
        compiler_params=_cparams(("arbitrary", "arbitrary")),
        name="oproj",
    )(o, w, x, mod, g)


def _ffn_kernel(hp_ref, h_ref, hx_ref, x_ref, mod_ref, wu_ref, cw_ref, cb_ref, wd_ref,
                o_ref, lhs_ref, *, tile, tiles_per_seq):
    ti = pl.program_id(1)
    halo = BF16_SUBLANES
    first = ti == 0
    last = ti == tiles_per_seq - 1
    lhs_ref[0:halo, :] = jnp.where(first, jnp.zeros_like(hp_ref[0]), hp_ref[0])
    lhs_ref[halo:halo + tile, :] = h_ref[0]
    lhs_ref[halo + tile:, :] = jnp.where(last, jnp.zeros_like(hx_ref[0]), hx_ref[0])
    lhs = lhs_ref[...]
    m = tile + 2 * halo

    def conv(u, c0):
        prev = pltpu.roll(u, 1, 0)[halo:halo + tile]
        nxt = pltpu.roll(u, m - 1, 0)[halo:halo + tile]
        cur = u[halo:halo + tile]
        return (prev * cw_ref[0:1, c0:c0 + FF_CHUNK] + cur * cw_ref[1:2, c0:c0 + FF_CHUNK]
                + nxt * cw_ref[2:3, c0:c0 + FF_CHUNK] + cb_ref[:, c0:c0 + FF_CHUNK])

    def up(j):
        cg = j * FF_CHUNK
        cv = D_FF + j * FF_CHUNK
        return (jnp.dot(lhs, wu_ref[:, cg:cg + FF_CHUNK], preferred_element_type=F32),
                jnp.dot(lhs, wu_ref[:, cv:cv + FF_CHUNK], preferred_element_type=F32))

    def gated(j, u):
        cg = j * FF_CHUNK
        gate = conv(u[0], cg)
        val = conv(u[1], D_FF + cg)
        return (gate * jax.nn.sigmoid(gate) * val).astype(BF16)

    n_chunks = D_FF // FF_CHUNK
    acc = None
    us, acts = {}, {}
    for step in range(n_chunks + 2):
        if step < n_chunks:
            us[step] = up(step)
        if 0 <= step - 1 < n_chunks:
            acts[step - 1] = gated(step - 1, us.pop(step - 1))
        if 0 <= step - 2 < n_chunks:
            cg = (step - 2) * FF_CHUNK
            part = jnp.dot(acts.pop(step - 2), wd_ref[cg:cg + FF_CHUNK, :],
                           preferred_element_type=F32)
            acc = part if acc is None else acc + part
    g2 = mod_ref[0, 5:6, :]
    o_ref[0] = x_ref[0] + g2 * acc


def _ffn(hn, x, mod, w_up, conv_w, conv_b, w_down, *, tile):
    b, l, d = x.shape
    nmod = mod.shape[0]
    nt = l // tile
    hb = tile // BF16_SUBLANES
    nhb = l // BF16_SUBLANES
    mod_map = (lambda bi, ti: (bi, 0, 0)) if nmod > 1 else (lambda bi, ti: (0, 0, 0))
    row = lambda bi, ti: (bi, ti, 0)
    const2 = lambda bi, ti: (0, 0)
    return pl.pallas_call(
        functools.partial(_ffn_kernel, tile=tile, tiles_per_seq=nt),
        grid=(b, nt),
        in_specs=[pl.BlockSpec((1, BF16_SUBLANES, d),
                               lambda bi, ti: (bi, jnp.maximum(ti * hb - 1, 0), 0)),
                  pl.BlockSpec((1, tile, d), row),
                  pl.BlockSpec((1, BF16_SUBLANES, d),
                               lambda bi, ti: (bi, jnp.minimum((ti + 1) * hb, nhb - 1), 0)),
                  pl.BlockSpec((1, tile, d), row),
                  pl.BlockSpec((1, MOD_ROWS, d), mod_map),
                  _resident((d, 2 * D_FF), const2),
                  pl.BlockSpec((CONV_W, 2 * D_FF), const2),
                  pl.BlockSpec((1, 2 * D_FF), const2),
                  _resident((D_FF, d), const2)],
        out_specs=pl.BlockSpec((1, tile, d), row),
        out_shape=jax.ShapeDtypeStruct((b, l, d), F32),
        scratch_shapes=[pltpu.VMEM((tile + 2 * BF16_SUBLANES, d), BF16)],
        input_output_aliases={3: 0},
        compiler_params=_cparams(("arbitrary", "arbitrary")),
        name="conv_ffn",
    )(hn, hn, hn, x, mod, w_up, conv_w, conv_b, w_down)


def _rope_tables(n_lat):
    t = jnp.arange(n_lat)
    row = (t // GRID_W).astype(F32)
    col = (t % GRID_W).astype(F32)
    n_freq = HEAD_DIM // 4
    inv_freq = ROPE_THETA ** (-jnp.arange(n_freq, dtype=F32) / n_freq)
    ang = jnp.concatenate([row[:, None] * inv_freq, col[:, None] * inv_freq], axis=-1)
    cos = jnp.cos(ang)
    sin = jnp.sin(ang)
    return (jnp.concatenate([cos, cos, cos, cos], axis=-1),
            jnp.concatenate([-sin, -sin, sin, sin], axis=-1))


def _pair_perm(n_heads):
    a = np.arange(QUARTER)
    blk = np.concatenate([a, a + HEAD_DIM, a + QUARTER, a + HEAD_DIM + QUARTER])
    return np.concatenate([p * LANES + blk for p in range(n_heads // 2)])


def _dup_perm(n_heads):
    a = np.arange(QUARTER)
    blk = np.concatenate([a, a, a + QUARTER, a + QUARTER])
    return np.concatenate([h * HEAD_DIM + blk for h in range(n_heads)])


def _head_mean_matrix():
    lane = np.arange(MXU_W)
    head = (lane // LANES) * 2 + (lane // QUARTER) % 2
    return jnp.asarray((head[:, None] == head[None, :]).astype(np.float32) / HEAD_DIM, dtype=BF16)


def _prep_qkv(w_qkv, qk_g, diff):
    n_q = D_MODEL
    q_idx = _pair_perm(n_q // HEAD_DIM)
    if diff:
        k_idx = _pair_perm(D_MODEL // HEAD_DIM)
        n_kv = D_MODEL
        w_v = w_qkv[:, n_q + n_kv:]
        v_add = None
    else:
        k_idx = _dup_perm(GQA_KV_HEADS)
        n_kv = GQA_KV_HEADS * HEAD_DIM
        w_v = w_qkv[:, n_q + n_kv:].reshape(D_MODEL, GQA_KV_HEADS, HEAD_DIM)
        w_v = jnp.pad(w_v, ((0, 0), (0, 0), (0, LANES - HEAD_DIM))).reshape(D_MODEL, -1)
        one_col = np.zeros((LANES,), np.float32)
        one_col[HEAD_DIM] = 1.0
        v_add = jnp.asarray(np.tile(one_col, GQA_KV_HEADS)).reshape(1, -1)
    w = jnp.concatenate([w_qkv[:, q_idx], w_qkv[:, n_q + k_idx], w_v], axis=1).astype(BF16)
    gain = jnp.concatenate([qk_g[0][q_idx % HEAD_DIM] * Q_SCALE, qk_g[1][k_idx % HEAD_DIM]])
    return w, gain.reshape(1, -1), v_add, n_q, n_q + k_idx.shape[0]


def kernel(x, c, ctx, c_ctx, adaln_w, adaln_b, norm1_g, norm2_g, ffn_w_up, ffn_conv_w, ffn_conv_b, ffn_w_down, a_w_qkv, a_qk_g, a_lambda, a_head_g, a_w_o, b_w_qkv, b_qk_g, b_w_o, c_w_qkv, c_qk_g, c_sink, c_w_o):
    b, n_lat, d = x.shape
    n_ctx = ctx.shape[1]
    depth = adaln_w.shape[0]
    assert d == D_MODEL and b + 1 <= C_ROWS

    tabs = _rope_tables(n_lat)
    bd = _head_mean_matrix()

    cvec = jnp.concatenate([c, c_ctx[None, :], jnp.zeros((C_ROWS - b - 1, d), F32)], axis=0)
    mod_all = _adaln(cvec, adaln_w, adaln_b).reshape(depth, C_ROWS, 6, d)
    mod_all = jnp.pad(mod_all, ((0, 0), (0, 0), (0, MOD_ROWS - 6), (0, 0)))

    lat_tile = min(512, n_lat)
    ctx_tile = n_ctx
    tq_lat = min(256, n_lat)
    tq_ctx = min(256, n_ctx)

    h_ctx = ctx
    for i in range(depth):
        last = i == depth - 1
        j = i // N_MIXERS
        kind = i % N_MIXERS
        mod_l = mod_all[i, :b]
        mod_c = mod_all[i, b:b + 1]
        g1n = norm1_g[i].reshape(1, d)
        g2n = norm2_g[i].reshape(1, d)
        if kind == 0:
            w_qkv, qk_g, w_o = a_w_qkv[j], a_qk_g[j], a_w_o[j]
        elif kind == 1:
            w_qkv, qk_g, w_o = b_w_qkv[j], b_qk_g[j], b_w_o[j]
        else:
            w_qkv, qk_g, w_o = c_w_qkv[j], c_qk_g[j], c_w_o[j]
        w_all, gain, v_add, n_q, n_norm = _prep_qkv(w_qkv, qk_g, kind == 0)
        w_o = w_o.astype(BF16)

        q_l, k_l, v_l = _qkv_proj(x, mod_l, g1n, w_all, gain, bd, tabs, v_add,
                                  n_q=n_q, n_norm=n_norm, tile=lat_tile)
        q_c, k_c, v_c = _qkv_proj(h_ctx, mod_c, g1n, w_all, gain, bd, None, v_add,
                                  n_q=n_q, n_norm=n_norm, tile=ctx_tile)

        o_c = None
        if kind == 0:
            lam_init = 0.8 - 0.6 * math.exp(-0.3 * i)
            hg = a_head_g[j].reshape(1, DIFF_V_DIM)
            o_l = _diff_attention(a_lambda[j], hg, q_l, [(k_l, v_l), (k_c, v_c)],
                                  lam_init=lam_init, tq=tq_lat)
            if not last:
                o_c = _diff_attention(a_lambda[j], hg, q_c, [(k_c, v_c)],
                                      lam_init=lam_init, tq=tq_ctx // 2)
        elif kind == 1:
            o_l = _gqa_dense_attention(q_l, [(k_l, v_l), (k_c, v_c)], tq=tq_lat)
            if not last:
                o_c = _gqa_dense_attention(q_c, [(k_c, v_c)], tq=tq_ctx)
        else:
            sink = c_sink[j].reshape(GQA_KV_HEADS, GQA_GROUP)
            o_l = _gqa_window_attention(q_l, k_l, v_l, k_c, v_c, sink, tq=tq_lat)
            if not last:
                o_c = _gqa_dense_attention(q_c, [(k_c, v_c)], tq=tq_ctx, sink=sink)

        w_up = ffn_w_up[i].astype(BF16)
        w_down = ffn_w_down[i].astype(BF16)
        conv_b = ffn_conv_b[i].reshape(1, 2 * D_FF)
        x, hn2 = _oproj(o_l, w_o, x, mod_l, g2n, tile=lat_tile)
        x = _ffn(hn2, x, mod_l, w_up, ffn_conv_w[i], conv_b, w_down, tile=lat_tile)
        if not last:
            h_ctx, hn2c = _oproj(o_c, w_o, h_ctx, mod_c, g2n, tile=ctx_tile)
            h_ctx = _ffn(hn2c, h_ctx, mod_c, w_up, ffn_conv_w[i], conv_b, w_down, tile=ctx_tile)
    return x
```

```python
import functools
import math

import numpy as np
import jax
import jax.numpy as jnp
from jax import lax
from jax.experimental import pallas as pl
from jax.experimental.pallas import tpu as pltpu

F32 = jnp.float32
BF16 = jnp.bfloat16

D_MODEL = 1024
HEAD_DIM = 64
GRID_W = 64
DIFF_HEADS = D_MODEL // (2 * HEAD_DIM)
DIFF_V_DIM = 2 * HEAD_DIM
GQA_HEADS = D_MODEL // HEAD_DIM
GQA_KV_HEADS = GQA_HEADS // 4
GQA_GROUP = GQA_HEADS // GQA_KV_HEADS
WINDOW = 128
D_FF = 256 * ((8 * D_MODEL // 3 + 255) // 256)
CONV_W = 3
ROPE_THETA = 10000.0
EPS = 1e-6
ATTN_SCALE = HEAD_DIM ** -0.5
LOG2E = 1.0 / math.log(2.0)
Q_SCALE = ATTN_SCALE * LOG2E
N_MIXERS = 3

LANES = 128
SUBLANES = 8
QUARTER = HEAD_DIM // 2
BF16_SUBLANES = 16
MXU_W = 256
FF_CHUNK = MXU_W
MOD_ROWS = 8
C_ROWS = 24
VMEM_LIMIT = 56 * 1024 * 1024
CAST_BLOCK_BYTES = 4 * 1024 * 1024
NEG_BIG = -1e30


def _cparams(sem, vmem=VMEM_LIMIT):
    return pltpu.CompilerParams(dimension_semantics=sem, vmem_limit_bytes=vmem)


def _resident(shape, index_map):
    return pl.BlockSpec(shape, index_map, pipeline_mode=pl.Buffered(1))


def _rms(x):
    return x * lax.rsqrt(jnp.mean(x * x, axis=-1, keepdims=True) + EPS)


def _head_a_mask():
    lane = lax.broadcasted_iota(jnp.int32, (1, LANES), 1)
    return (lane & QUARTER) == 0


def _cast_kernel(x_ref, o_ref):
    o_ref[...] = x_ref[...].astype(BF16)


def _to_bf16(w):
    n, r, c = w.shape
    parts = 1
    while (r // parts) * c * 4 > CAST_BLOCK_BYTES or r % parts or (r // parts) % BF16_SUBLANES:
        parts += 1
    tr = r // parts
    blk = lambda i, j: (i, j, 0)
    return pl.pallas_call(
        _cast_kernel,
        grid=(n, parts),
        in_specs=[pl.BlockSpec((1, tr, c), blk)],
        out_specs=pl.BlockSpec((1, tr, c), blk),
        out_shape=jax.ShapeDtypeStruct(w.shape, BF16),
        compiler_params=_cparams(("arbitrary", "arbitrary"), 32 * 1024 * 1024),
        name="cast_bf16",
    )(w)


def _adaln_kernel(c_ref, w_ref, b_ref, o_ref):
    c = c_ref[...]
    s = c * jax.nn.sigmoid(c)
    o_ref[0] = jnp.dot(s.astype(BF16), w_ref[0].astype(BF16),
                       preferred_element_type=F32) + b_ref[0]


def _adaln(cvec, adaln_w, adaln_b):
    depth, d, n = adaln_w.shape
    tn = 1536
    return pl.pallas_call(
        _adaln_kernel,
        grid=(depth, n // tn),
        in_specs=[pl.BlockSpec((C_ROWS, d), lambda i, j: (0, 0)),
                  pl.BlockSpec((1, d, tn), lambda i, j: (i, 0, j)),
                  pl.BlockSpec((1, 1, tn), lambda i, j: (i, 0, j))],
        out_specs=pl.BlockSpec((1, C_ROWS, tn), lambda i, j: (i, 0, j)),
        out_shape=jax.ShapeDtypeStruct((depth, C_ROWS, n), F32),
        compiler_params=_cparams(("arbitrary", "arbitrary"), 40 * 1024 * 1024),
        name="adaln",
    )(cvec, adaln_w, adaln_b.reshape(depth, 1, n))


def _qkv_kernel(*refs, rope, n_q, n_norm, v_add):
    x_ref, mod_ref, g_ref, w_ref, gain_ref, bd_ref = refs[:6]
    pos = 6
    if rope:
        cos = refs[pos][...]
        sin = refs[pos + 1][...]
        pos += 2
    if v_add:
        va_ref = refs[pos]
        pos += 1
    q_ref, k_ref, v_ref = refs[pos:]
    n_total = w_ref.shape[1]

    x = x_ref[0]
    sh = mod_ref[0, 0:1, :]
    sc = mod_ref[0, 1:2, :]
    hn = ((_rms(x) * g_ref[...]) * (1.0 + sc) + sh).astype(BF16)

    def project(c0):
        return jnp.dot(hn, w_ref[:, c0:c0 + MXU_W], preferred_element_type=F32)

    def mean_sq(z):
        return jnp.dot((z * z).astype(BF16), bd_ref[...], preferred_element_type=F32)

    def finish(c0, z, msq):
        if c0 < n_norm:
            z = z * lax.rsqrt(msq + EPS) * gain_ref[:, c0:c0 + MXU_W]
            if rope:
                z = jnp.concatenate(
                    [z[:, s0:s0 + LANES] * cos + pltpu.roll(z[:, s0:s0 + LANES], HEAD_DIM, 1) * sin
                     for s0 in range(0, MXU_W, LANES)], axis=-1)
            if c0 < n_q:
                q_ref[0, :, c0:c0 + MXU_W] = z.astype(BF16)
            else:
                k_ref[0, :, c0 - n_q:c0 - n_q + MXU_W] = z.astype(BF16)
        else:
            if v_add:
                z = z + va_ref[:, c0 - n_norm:c0 - n_norm + MXU_W]
            v_ref[0, :, c0 - n_norm:c0 - n_norm + MXU_W] = z.astype(BF16)

    starts = list(range(0, n_total, MXU_W))
    zs, ms = {}, {}
    for step in range(len(starts) + 2):
        if step < len(starts):
            zs[step] = project(starts[step])
        if 0 <= step - 1 < len(starts) and starts[step - 1] < n_norm:
            ms[step - 1] = mean_sq(zs[step - 1])
        if 0 <= step - 2 < len(starts):
            finish(starts[step - 2], zs.pop(step - 2), ms.pop(step - 2, None))


def _qkv_proj(x, mod, g, w, gain, bd, tabs, v_add, *, n_q, n_norm, tile):
    b, l, d = x.shape
    nmod = mod.shape[0]
    n = w.shape[1]
    n_k = n_norm - n_q
    n_v = n - n_norm
    rope = tabs is not None
    mod_map = (lambda bi, ti: (bi, 0, 0)) if nmod > 1 else (lambda bi, ti: (0, 0, 0))
    const2 = lambda bi, ti: (0, 0)
    row = lambda bi, ti: (bi, ti, 0)
    in_specs = [pl.BlockSpec((1, tile, d), row),
                pl.BlockSpec((1, MOD_ROWS, d), mod_map),
                pl.BlockSpec((1, d), const2),
                _resident((d, n), const2),
                pl.BlockSpec((1, n_norm), const2),
                pl.BlockSpec((MXU_W, MXU_W), const2)]
    args = [x, mod, g, w, gain, bd]
    if rope:
        in_specs += [pl.BlockSpec((tile, LANES), lambda bi, ti: (ti, 0))] * 2
        args += list(tabs)
    if v_add is not None:
        in_specs.append(pl.BlockSpec((1, n_v), const2))
        args.append(v_add)
    return pl.pallas_call(
        functools.partial(_qkv_kernel, rope=rope, n_q=n_q, n_norm=n_norm, v_add=v_add is not None),
        grid=(b, l // tile),
        in_specs=in_specs,
        out_specs=[pl.BlockSpec((1, tile, n_q), row), pl.BlockSpec((1, tile, n_k), row),
                   pl.BlockSpec((1, tile, n_v), row)],
        out_shape=[jax.ShapeDtypeStruct((b, l, n_q), BF16), jax.ShapeDtypeStruct((b, l, n_k), BF16),
                   jax.ShapeDtypeStruct((b, l, n_v), BF16)],
        compiler_params=_cparams(("arbitrary", "arbitrary")),
        name="qkv_proj",
    )(*args)


def _dot_nt(a, b):
    return lax.dot_general(a, b, (((1,), (1,)), ((), ())), preferred_element_type=F32)


def _row_max(s_list):
    m = s_list[0].max(axis=-1, keepdims=True)
    for s in s_list[1:]:
        m = jnp.maximum(m, s.max(axis=-1, keepdims=True))
    return m


def _split_heads(qp):
    mask_a = _head_a_mask()
    zero = jnp.zeros_like(qp)
    return jnp.where(mask_a, qp, zero), jnp.where(mask_a, zero, qp)


def _softmax_pv(s_list, v_list, sink_logit, dv):
    m = _row_max(s_list)
    if sink_logit is not None:
        m = jnp.maximum(m, sink_logit)
    o = None
    for s, v in zip(s_list, v_list):
        oj = jnp.dot(jnp.exp2(s - m).astype(BF16), v, preferred_element_type=F32)
        o = oj if o is None else o + oj
    den = o[:, dv:dv + 1]
    if sink_logit is not None:
        den = den + jnp.exp2(sink_logit - m)
    return o[:, :dv] * (1.0 / den)


def _head_tiles(q_ref, tq, per_trip, tile_ctx, logits, finish, emit):
    nq = q_ref.shape[1] // tq
    assert nq % per_trip == 0

    def heads(r0):
        hs = []
        for p0 in range(0, q_ref.shape[2], LANES):
            hs += list(_split_heads(q_ref[0, pl.ds(r0, tq), p0:p0 + LANES]))
        return hs

    def body(i, carry):
        pending = None
        outs = []
        r_prev = None
        for u in range(per_trip):
            r0 = pl.multiple_of((i * per_trip + u) * tq, tq)
            ctx = tile_ctx(r0)
            for g, q in enumerate(heads(r0)):
                s = logits(ctx, q)
                if pending is not None:
                    outs.append(finish(*pending))
                pending = (ctx, s, g)
                if g == 0 and r_prev is not None:
                    emit(r_prev, outs)
                    outs = []
            r_prev = r0
        outs.append(finish(*pending))
        emit(r_prev, outs)
        return carry

    lax.fori_loop(0, nq // per_trip, body, 0)


def _diff_attn_kernel(*refs, lam_init, n_src, tq, per_trip):
    lam_ref, hg_ref, q_ref = refs[:3]
    kv_refs = refs[3:3 + 2 * n_src]
    o_ref = refs[3 + 2 * n_src]
    va_refs = refs[4 + 2 * n_src:]
    lq = lam_ref[...]
    lam = (jnp.exp(jnp.sum(lq[0:1] * lq[1:2], axis=-1, keepdims=True))
           - jnp.exp(jnp.sum(lq[2:3] * lq[3:4], axis=-1, keepdims=True)) + lam_init)
    hg = hg_ref[...] * (1.0 - lam_init)
    lane = lax.broadcasted_iota(jnp.int32, (1, LANES), 1)
    for j in range(n_src):
        s = kv_refs[2 * j].shape[1]
        va_refs[j][:, :DIFF_V_DIM] = kv_refs[2 * j + 1][0]
        va_refs[j][:, DIFF_V_DIM:] = jnp.broadcast_to(
            jnp.where(lane == 0, 1.0, 0.0).astype(BF16), (s, LANES))

    def logits(ctx, q):
        return [_dot_nt(q, kv_refs[2 * j][0]) for j in range(n_src)]

    def finish(ctx, s, g):
        return _softmax_pv(s, [va_refs[j][...] for j in range(n_src)], None, DIFF_V_DIM)

    def emit(r0, outs):
        o = outs[0] - lam * outs[1]
        o_ref[0, pl.ds(r0, tq), :] = (_rms(o) * hg).astype(BF16)

    _head_tiles(q_ref, tq, per_trip, lambda r0: None, logits, finish, emit)


def _diff_attention(lam_p, head_g, q, kvs, *, lam_init, tq, per_trip):
    b, lq, d = q.shape
    n_src = len(kvs)
    blk = lambda bi, h: (bi, 0, h)
    in_specs = [pl.BlockSpec((4, HEAD_DIM), lambda bi, h: (0, 0)),
                pl.BlockSpec((1, DIFF_V_DIM), lambda bi, h: (0, 0)),
                pl.BlockSpec((1, lq, LANES), blk)]
    args = [lam_p, head_g, q]
    for k, v in kvs:
        in_specs += [pl.BlockSpec((1, k.shape[1], LANES), blk)] * 2
        args += [k, v]
    return pl.pallas_call(
        functools.partial(_diff_attn_kernel, lam_init=lam_init, n_src=n_src, tq=tq,
                          per_trip=per_trip),
        grid=(b, DIFF_HEADS),
        in_specs=in_specs,
        out_specs=pl.BlockSpec((1, lq, LANES), blk),
        out_shape=jax.ShapeDtypeStruct((b, lq, d), BF16),
        scratch_shapes=[pltpu.VMEM((k.shape[1], DIFF_V_DIM + LANES), BF16) for k, _ in kvs],
        compiler_params=_cparams(("arbitrary", "arbitrary")),
        name="diff_attn",
    )(*args)


def _gqa_emit(o_ref, tq):
    def emit(r0, outs):
        o_ref[0, pl.ds(r0, tq), :] = jnp.concatenate(outs, axis=-1).astype(BF16)
    return emit


def _gqa_dense_kernel(*refs, n_src, tq, per_trip, sink):
    if sink:
        sink_ref = refs[0]
        refs = refs[1:]
    q_ref = refs[0]
    kv_refs = refs[1:1 + 2 * n_src]
    o_ref = refs[1 + 2 * n_src]
    kk = pl.program_id(1)

    def logits(ctx, q):
        return [_dot_nt(q, kv_refs[2 * j][0]) for j in range(n_src)]

    def finish(ctx, s, g):
        v = [kv_refs[2 * j + 1][0] for j in range(n_src)]
        return _softmax_pv(s, v, sink_ref[kk, g] * LOG2E if sink else None, HEAD_DIM)

    _head_tiles(q_ref, tq, per_trip, lambda r0: None, logits, finish, _gqa_emit(o_ref, tq))


def _gqa_dense_attention(q, kvs, *, tq, per_trip, sink=None):
    b, lq, d = q.shape
    n_src = len(kvs)
    in_specs = []
    args = []
    if sink is not None:
        in_specs.append(pl.BlockSpec(memory_space=pltpu.SMEM))
        args.append(sink)
    blk = lambda bi, kk: (bi, 0, kk)
    in_specs.append(pl.BlockSpec((1, lq, GQA_GROUP * HEAD_DIM), blk))
    args.append(q)
    for k, v in kvs:
        in_specs += [pl.BlockSpec((1, k.shape[1], LANES), blk)] * 2
        args += [k, v]
    return pl.pallas_call(
        functools.partial(_gqa_dense_kernel, n_src=n_src, tq=tq, per_trip=per_trip,
                          sink=sink is not None),
        grid=(b, GQA_KV_HEADS),
        in_specs=in_specs,
        out_specs=pl.BlockSpec((1, lq, GQA_GROUP * HEAD_DIM), blk),
        out_shape=jax.ShapeDtypeStruct((b, lq, d), BF16),
        compiler_params=_cparams(("arbitrary", "arbitrary")),
        name="gqa_dense_attn",
    )(*args)


def _gqa_window_kernel(sink_ref, q_ref, kl_ref, vl_ref, kc_ref, vc_ref, o_ref, *, tq, per_trip):
    n_lat = q_ref.shape[1]
    band = min(tq + 2 * WINDOW, n_lat)
    kk = pl.program_id(1)
    rel0 = (lax.broadcasted_iota(jnp.int32, (tq, band), 1)
            - lax.broadcasted_iota(jnp.int32, (tq, band), 0))

    def tile_ctx(r0):
        start = pl.multiple_of(jnp.clip(r0 - WINDOW, 0, n_lat - band), WINDOW)
        kb = kl_ref[0, pl.ds(start, band), :]
        vb = vl_ref[0, pl.ds(start, band), :]
        in_band = jnp.abs(rel0 + (start - r0)) <= WINDOW
        return kb, vb, in_band

    def logits(ctx, q):
        kb, _, in_band = ctx
        return [jnp.where(in_band, _dot_nt(q, kb), NEG_BIG), _dot_nt(q, kc_ref[0])]

    def finish(ctx, s, g):
        return _softmax_pv(s, [ctx[1], vc_ref[0]], sink_ref[kk, g] * LOG2E, HEAD_DIM)

    _head_tiles(q_ref, tq, per_trip, tile_ctx, logits, finish, _gqa_emit(o_ref, tq))


def _gqa_window_attention(q, k_l, v_l, k_c, v_c, sink, *, tq, per_trip):
    b, lq, d = q.shape
    lc = k_c.shape[1]
    blk = lambda bi, kk: (bi, 0, kk)
    return pl.pallas_call(
        functools.partial(_gqa_window_kernel, tq=tq, per_trip=per_trip),
        grid=(b, GQA_KV_HEADS),
        in_specs=[pl.BlockSpec(memory_space=pltpu.SMEM),
                  pl.BlockSpec((1, lq, GQA_GROUP * HEAD_DIM), blk),
                  pl.BlockSpec((1, lq, LANES), blk),
                  pl.BlockSpec((1, lq, LANES), blk),
                  pl.BlockSpec((1, lc, LANES), blk),
                  pl.BlockSpec((1, lc, LANES), blk)],
        out_specs=pl.BlockSpec((1, lq, GQA_GROUP * HEAD_DIM), blk),
        out_shape=jax.ShapeDtypeStruct((b, lq, d), BF16),
        compiler_params=_cparams(("arbitrary", "arbitrary")),
        name="gqa_window_attn",
    )(sink, q, k_l, v_l, k_c, v_c)


def _tail_kernel(op_ref, o_ref, ox_ref, xp_ref, x_ref, xx_ref, mod_ref, g_ref, wo_ref, wu_ref,
                 cw_ref, cb_ref, wd_ref, out_ref, olhs_ref, *, tile, tiles_per_seq):
    ti = pl.program_id(1)
    oh = BF16_SUBLANES
    xh = SUBLANES
    m = tile + 2 * xh
    olhs_ref[0:oh, :] = op_ref[0]
    olhs_ref[oh:oh + tile, :] = o_ref[0]
    olhs_ref[oh + tile:, :] = ox_ref[0]
    y = jnp.dot(olhs_ref[...], wo_ref[0], preferred_element_type=F32)[oh - xh:oh + tile + xh]
    g1 = mod_ref[0, 2:3, :]
    sh2 = mod_ref[0, 3:4, :]
    sc2 = mod_ref[0, 4:5, :]
    g2 = mod_ref[0, 5:6, :]
    xn = jnp.concatenate([xp_ref[0], x_ref[0], xx_ref[0]], axis=0) + g1 * y
    hn = (_rms(xn) * g_ref[...]) * (1.0 + sc2) + sh2
    row = lax.broadcasted_iota(jnp.int32, (m, 1), 0)
    keep = jnp.logical_and(jnp.logical_or(ti > 0, row >= xh),
                           jnp.logical_or(ti < tiles_per_seq - 1, row < xh + tile))
    lhs = jnp.where(keep, hn, 0.0).astype(BF16)

    def conv(u, c0):
        prev = pltpu.roll(u, 1, 0)[xh:xh + tile]
        nxt = pltpu.roll(u, m - 1, 0)[xh:xh + tile]
        cur = u[xh:xh + tile]
        return (prev * cw_ref[0, 0:1, c0:c0 + FF_CHUNK] + cur * cw_ref[0, 1:2, c0:c0 + FF_CHUNK]
                + nxt * cw_ref[0, 2:3, c0:c0 + FF_CHUNK] + cb_ref[0, :, c0:c0 + FF_CHUNK])

    def up(j):
        cg = j * FF_CHUNK
        cv = D_FF + j * FF_CHUNK
        return (jnp.dot(lhs, wu_ref[0, :, cg:cg + FF_CHUNK], preferred_element_type=F32),
                jnp.dot(lhs, wu_ref[0, :, cv:cv + FF_CHUNK], preferred_element_type=F32))

    def gated(j, u):
        cg = j * FF_CHUNK
        gate = conv(u[0], cg)
        val = conv(u[1], D_FF + cg)
        return (gate * jax.nn.sigmoid(gate) * val).astype(BF16)

    n_chunks = D_FF // FF_CHUNK
    acc = None
    us, acts = {}, {}
    for step in range(n_chunks + 2):
        if step < n_chunks:
            us[step] = up(step)
        if 0 <= step - 1 < n_chunks:
            acts[step - 1] = gated(step - 1, us.pop(step - 1))
        if 0 <= step - 2 < n_chunks:
            cg = (step - 2) * FF_CHUNK
            part = jnp.dot(acts.pop(step - 2), wd_ref[0, cg:cg + FF_CHUNK, :],
                           preferred_element_type=F32)
            acc = part if acc is None else acc + part
    out_ref[0] = xn[xh:xh + tile] + g2 * acc


def _tail(o, x, mod, g2n, w_o, wo_layer, w_up, conv_w, conv_b, w_down, layer, *, tile, alias):
    b, l, d = x.shape
    nmod = mod.shape[0]
    nt = l // tile
    ob = tile // BF16_SUBLANES
    xb = tile // SUBLANES
    mod_map = (lambda bi, ti: (bi, 0, 0)) if nmod > 1 else (lambda bi, ti: (0, 0, 0))
    row = lambda bi, ti: (bi, ti, 0)
    lay = lambda bi, ti: (layer, 0, 0)
    lay_o = lambda bi, ti: (wo_layer, 0, 0)
    prev = lambda nb: (lambda bi, ti: (bi, jnp.maximum(ti * nb - 1, 0), 0))
    nxt = lambda nb: (lambda bi, ti: (bi, jnp.minimum((ti + 1) * nb, nt * nb - 1), 0))
    return pl.pallas_call(
        functools.partial(_tail_kernel, tile=tile, tiles_per_seq=nt),
        grid=(b, nt),
        in_specs=[pl.BlockSpec((1, BF16_SUBLANES, d), prev(ob)),
                  pl.BlockSpec((1, tile, d), row),
                  pl.BlockSpec((1, BF16_SUBLANES, d), nxt(ob)),
                  pl.BlockSpec((1, SUBLANES, d), prev(xb)),
                  pl.BlockSpec((1, tile, d), row),
                  pl.BlockSpec((1, SUBLANES, d), nxt(xb)),
                  pl.BlockSpec((1, MOD_ROWS, d), mod_map),
                  pl.BlockSpec((1, d), lambda bi, ti: (0, 0)),
                  _resident((1, d, d), lay_o),
                  _resident((1, d, 2 * D_FF), lay),
                  pl.BlockSpec((1, CONV_W, 2 * D_FF), lay),
                  pl.BlockSpec((1, 1, 2 * D_FF), lay),
                  _resident((1, D_FF, d), lay)],
        out_specs=pl.BlockSpec((1, tile, d), row),
        out_shape=jax.ShapeDtypeStruct((b, l, d), F32),
        scratch_shapes=[pltpu.VMEM((tile + 2 * BF16_SUBLANES, d), BF16)],
        input_output_aliases={4: 0} if alias else {},
        compiler_params=_cparams(("arbitrary", "arbitrary")),
        name="layer_tail",
    )(o, o, o, x, x, x, mod, g2n, w_o, w_up, conv_w, conv_b, w_down)


def _rope_tables(n_lat):
    t = jnp.arange(n_lat)
    row = (t // GRID_W).astype(F32)
    col = (t % GRID_W).astype(F32)
    n_freq = HEAD_DIM // 4
    inv_freq = ROPE_THETA ** (-jnp.arange(n_freq, dtype=F32) / n_freq)
    ang = jnp.concatenate([row[:, None] * inv_freq, col[:, None] * inv_freq], axis=-1)
    cos = jnp.cos(ang)
    sin = jnp.sin(ang)
    return (jnp.concatenate([cos, cos, cos, cos], axis=-1),
            jnp.concatenate([-sin, -sin, sin, sin], axis=-1))


def _pair_layout(w):
    lead = w.shape[:-1]
    w = w.reshape(lead + (-1, 2, 2, QUARTER))
    return jnp.swapaxes(w, -3, -2).reshape(lead + (-1,))


def _dup_layout(w):
    lead = w.shape[:-1]
    w = w.reshape(lead + (-1, 2, 1, QUARTER))
    return jnp.broadcast_to(w, w.shape[:-2] + (2, QUARTER)).reshape(lead + (-1,))


def _head_mean_matrix():
    lane = np.arange(MXU_W)
    head = (lane // LANES) * 2 + (lane // QUARTER) % 2
    return jnp.asarray((head[:, None] == head[None, :]).astype(np.float32) / HEAD_DIM, dtype=BF16)


def _prep_qkv(w_qkv, qk_g, diff):
    n_q = D_MODEL
    w_q = _pair_layout(w_qkv[:, :n_q])
    if diff:
        n_kv = D_MODEL
        w_k = _pair_layout(w_qkv[:, n_q:n_q + n_kv])
        w_v = w_qkv[:, n_q + n_kv:]
        v_add = None
    else:
        n_kv = GQA_KV_HEADS * HEAD_DIM
        w_k = _dup_layout(w_qkv[:, n_q:n_q + n_kv])
        w_v = w_qkv[:, n_q + n_kv:].reshape(D_MODEL, GQA_KV_HEADS, HEAD_DIM)
        w_v = jnp.pad(w_v, ((0, 0), (0, 0), (0, LANES - HEAD_DIM))).reshape(D_MODEL, -1)
        one_col = np.zeros((LANES,), np.float32)
        one_col[HEAD_DIM] = 1.0
        v_add = jnp.asarray(np.tile(one_col, GQA_KV_HEADS)).reshape(1, -1)
    w = jnp.concatenate([w_q, w_k, w_v], axis=1)
    gain = jnp.concatenate([jnp.tile(_dup_layout(qk_g[0] * Q_SCALE), n_q // LANES),
                            jnp.tile(_dup_layout(qk_g[1]), w_k.shape[1] // LANES)])
    return w, gain.reshape(1, -1), v_add, n_q, n_q + w_k.shape[1]


def kernel(x, c, ctx, c_ctx, adaln_w, adaln_b, norm1_g, norm2_g, ffn_w_up, ffn_conv_w, ffn_conv_b, ffn_w_down, a_w_qkv, a_qk_g, a_lambda, a_head_g, a_w_o, b_w_qkv, b_qk_g, b_w_o, c_w_qkv, c_qk_g, c_sink, c_w_o):
    b, n_lat, d = x.shape
    n_ctx = ctx.shape[1]
    depth = adaln_w.shape[0]
    assert d == D_MODEL and b + 1 <= C_ROWS

    tabs = _rope_tables(n_lat)
    bd = _head_mean_matrix()

    cvec = jnp.concatenate([c, c_ctx[None, :], jnp.zeros((C_ROWS - b - 1, d), F32)], axis=0)
    mod_all = _adaln(cvec, adaln_w, adaln_b).reshape(depth, C_ROWS, 6, d)
    mod_all = jnp.pad(mod_all, ((0, 0), (0, 0), (0, MOD_ROWS - 6), (0, 0)))

    w_up_all = _to_bf16(ffn_w_up)
    w_down_all = _to_bf16(ffn_w_down)
    conv_b_all = ffn_conv_b.reshape(depth, 1, 2 * D_FF)
    qkv_all = [_to_bf16(w) for w in (a_w_qkv, b_w_qkv, c_w_qkv)]
    wo_all = [_to_bf16(w) for w in (a_w_o, b_w_o, c_w_o)]
    qk_g_all = (a_qk_g, b_qk_g, c_qk_g)

    qkv_tile = min(512, n_lat)
    tail_tile = min(256, n_lat)
    tq_lat = min(256, n_lat)
    tq_ctx = min(256, n_ctx)
    trip = lambda lq, tq, want: max(p for p in (1, 2, 4) if p <= want and (lq // tq) % p == 0)

    h_ctx = ctx
    for i in range(depth):
        last = i == depth - 1
        j = i // N_MIXERS
        kind = i % N_MIXERS
        mod_l = mod_all[i, :b]
        mod_c = mod_all[i, b:b + 1]
        g1n = norm1_g[i].reshape(1, d)
        g2n = norm2_g[i].reshape(1, d)
        w_all, gain, v_add, n_q, n_norm = _prep_qkv(qkv_all[kind][j], qk_g_all[kind][j], kind == 0)

        q_l, k_l, v_l = _qkv_proj(x, mod_l, g1n, w_all, gain, bd, tabs, v_add,
                                  n_q=n_q, n_norm=n_norm, tile=qkv_tile)
        q_c, k_c, v_c = _qkv_proj(h_ctx, mod_c, g1n, w_all, gain, bd, None, v_add,
                                  n_q=n_q, n_norm=n_norm, tile=n_ctx)

        o_c = None
        if kind == 0:
            lam_init = 0.8 - 0.6 * math.exp(-0.3 * i)
            hg = a_head_g[j].reshape(1, DIFF_V_DIM)
            o_l = _diff_attention(a_lambda[j], hg, q_l, [(k_l, v_l), (k_c, v_c)], lam_init=lam_init,
                                  tq=tq_lat, per_trip=trip(n_lat, tq_lat, 4))
            if not last:
                o_c = _diff_attention(a_lambda[j], hg, q_c, [(k_c, v_c)], lam_init=lam_init,
                                      tq=tq_ctx // 2, per_trip=2)
        elif kind == 1:
            o_l = _gqa_dense_attention(q_l, [(k_l, v_l), (k_c, v_c)], tq=tq_lat,
                                       per_trip=trip(n_lat, tq_lat, 2))
            if not last:
                o_c = _gqa_dense_attention(q_c, [(k_c, v_c)], tq=tq_ctx, per_trip=1)
        else:
            sink = c_sink[j].reshape(GQA_KV_HEADS, GQA_GROUP)
            o_l = _gqa_window_attention(q_l, k_l, v_l, k_c, v_c, sink, tq=tq_lat,
                                        per_trip=trip(n_lat, tq_lat, 2))
            if not last:
                o_c = _gqa_dense_attention(q_c, [(k_c, v_c)], tq=tq_ctx, per_trip=1, sink=sink)

        tail = functools.partial(_tail, g2n=g2n, w_o=wo_all[kind], wo_layer=j, w_up=w_up_all,
                                 conv_w=ffn_conv_w, conv_b=conv_b_all, w_down=w_down_all, layer=i,
                                 alias=i > 0)
        x = tail(o_l, x, mod_l, tile=tail_tile)
        if not last:
            h_ctx = tail(o_c, h_ctx, mod_c, tile=n_ctx)
    return x
```

```python
import functools
import math

import numpy as np
import jax
import jax.numpy as jnp
from jax import lax
from jax.experimental import pallas as pl
from jax.experimental.pallas import tpu as pltpu

F32 = jnp.float32
BF16 = jnp.bfloat16

D_MODEL = 1024
HEAD_DIM = 64
GRID_W = 64
DIFF_HEADS = D_MODEL // (2 * HEAD_DIM)
DIFF_V_DIM = 2 * HEAD_DIM
GQA_HEADS = D_MODEL // HEAD_DIM
GQA_KV_HEADS = GQA_HEADS // 4
GQA_GROUP = GQA_HEADS // GQA_KV_HEADS
WINDOW = 128
D_FF = 256 * ((8 * D_MODEL // 3 + 255) // 256)
CONV_W = 3
ROPE_THETA = 10000.0
EPS = 1e-6
ATTN_SCALE = HEAD_DIM ** -0.5
LOG2E = 1.0 / math.log(2.0)
Q_SCALE = ATTN_SCALE * LOG2E
N_MIXERS = 3

LANES = 128
SUBLANES = 8
QUARTER = HEAD_DIM // 2
BF16_SUBLANES = 16
MXU_W = 256
FF_CHUNK = MXU_W
TAIL_SUB = 256
MOD_ROWS = 8
C_ROWS = 24
VMEM_LIMIT = 56 * 1024 * 1024
CAST_BLOCK_BYTES = 4 * 1024 * 1024
NEG_BIG = -1e30


def _cparams(sem, vmem=VMEM_LIMIT):
    return pltpu.CompilerParams(dimension_semantics=sem, vmem_limit_bytes=vmem)


def _resident(shape, index_map):
    return pl.BlockSpec(shape, index_map, pipeline_mode=pl.Buffered(1))


def _rms(x):
    return x * lax.rsqrt(jnp.mean(x * x, axis=-1, keepdims=True) + EPS)


def _head_a_mask():
    lane = lax.broadcasted_iota(jnp.int32, (1, LANES), 1)
    return (lane & QUARTER) == 0


def _cast_kernel(x_ref, o_ref):
    o_ref[...] = x_ref[...].astype(BF16)


def _to_bf16(w):
    n, r, c = w.shape
    parts = 1
    while (r // parts) * c * 4 > CAST_BLOCK_BYTES or r % parts or (r // parts) % BF16_SUBLANES:
        parts += 1
    tr = r // parts
    blk = lambda i, j: (i, j, 0)
    return pl.pallas_call(
        _cast_kernel,
        grid=(n, parts),
        in_specs=[pl.BlockSpec((1, tr, c), blk)],
        out_specs=pl.BlockSpec((1, tr, c), blk),
        out_shape=jax.ShapeDtypeStruct(w.shape, BF16),
        compiler_params=_cparams(("arbitrary", "arbitrary"), 32 * 1024 * 1024),
        name="cast_bf16",
    )(w)


def _adaln_kernel(c_ref, w_ref, b_ref, o_ref):
    c = c_ref[...]
    s = c * jax.nn.sigmoid(c)
    o_ref[0] = jnp.dot(s.astype(BF16), w_ref[0].astype(BF16),
                       preferred_element_type=F32) + b_ref[0]


def _adaln(cvec, adaln_w, adaln_b):
    depth, d, n = adaln_w.shape
    tn = 1536
    return pl.pallas_call(
        _adaln_kernel,
        grid=(depth, n // tn),
        in_specs=[pl.BlockSpec((C_ROWS, d), lambda i, j: (0, 0)),
                  pl.BlockSpec((1, d, tn), lambda i, j: (i, 0, j)),
                  pl.BlockSpec((1, 1, tn), lambda i, j: (i, 0, j))],
        out_specs=pl.BlockSpec((1, C_ROWS, tn), lambda i, j: (i, 0, j)),
        out_shape=jax.ShapeDtypeStruct((depth, C_ROWS, n), F32),
        compiler_params=_cparams(("arbitrary", "arbitrary"), 40 * 1024 * 1024),
        name="adaln",
    )(cvec, adaln_w, adaln_b.reshape(depth, 1, n))


def _qkv_kernel(*refs, rope, n_q, n_norm, v_add):
    x_ref, mod_ref, g_ref, w_ref, gain_ref, bd_ref = refs[:6]
    pos = 6
    if rope:
        cos = refs[pos][...]
        sin = refs[pos + 1][...]
        pos += 2
    if v_add:
        va_ref = refs[pos]
        pos += 1
    q_ref, k_ref, v_ref = refs[pos:]
    n_total = w_ref.shape[1]

    x = x_ref[0]
    sh = mod_ref[0, 0:1, :]
    sc = mod_ref[0, 1:2, :]
    hn = ((_rms(x) * g_ref[...]) * (1.0 + sc) + sh).astype(BF16)

    def project(c0):
        return jnp.dot(hn, w_ref[:, c0:c0 + MXU_W], preferred_element_type=F32)

    def mean_sq(z):
        return jnp.dot((z * z).astype(BF16), bd_ref[...], preferred_element_type=F32)

    def finish(c0, z, msq):
        if c0 < n_norm:
            z = z * lax.rsqrt(msq + EPS) * gain_ref[:, c0:c0 + MXU_W]
            if rope:
                z = jnp.concatenate(
                    [z[:, s0:s0 + LANES] * cos + pltpu.roll(z[:, s0:s0 + LANES], HEAD_DIM, 1) * sin
                     for s0 in range(0, MXU_W, LANES)], axis=-1)
            if c0 < n_q:
                q_ref[0, :, c0:c0 + MXU_W] = z.astype(BF16)
            else:
                k_ref[0, :, c0 - n_q:c0 - n_q + MXU_W] = z.astype(BF16)
        else:
            if v_add:
                z = z + va_ref[:, c0 - n_norm:c0 - n_norm + MXU_W]
            v_ref[0, :, c0 - n_norm:c0 - n_norm + MXU_W] = z.astype(BF16)

    starts = list(range(0, n_total, MXU_W))
    zs, ms = {}, {}
    for step in range(len(starts) + 2):
        if step < len(starts):
            zs[step] = project(starts[step])
        if 0 <= step - 1 < len(starts) and starts[step - 1] < n_norm:
            ms[step - 1] = mean_sq(zs[step - 1])
        if 0 <= step - 2 < len(starts):
            finish(starts[step - 2], zs.pop(step - 2), ms.pop(step - 2, None))


def _qkv_proj(x, mod, g, w, gain, bd, tabs, v_add, *, n_q, n_norm, tile):
    b, l, d = x.shape
    nmod = mod.shape[0]
    n = w.shape[1]
    n_k = n_norm - n_q
    n_v = n - n_norm
    rope = tabs is not None
    mod_map = (lambda bi, ti: (bi, 0, 0)) if nmod > 1 else (lambda bi, ti: (0, 0, 0))
    const2 = lambda bi, ti: (0, 0)
    row = lambda bi, ti: (bi, ti, 0)
    in_specs = [pl.BlockSpec((1, tile, d), row),
                pl.BlockSpec((1, MOD_ROWS, d), mod_map),
                pl.BlockSpec((1, d), const2),
                _resident((d, n), const2),
                pl.BlockSpec((1, n_norm), const2),
                pl.BlockSpec((MXU_W, MXU_W), const2)]
    args = [x, mod, g, w, gain, bd]
    if rope:
        in_specs += [pl.BlockSpec((tile, LANES), lambda bi, ti: (ti, 0))] * 2
        args += list(tabs)
    if v_add is not None:
        in_specs.append(pl.BlockSpec((1, n_v), const2))
        args.append(v_add)
    return pl.pallas_call(
        functools.partial(_qkv_kernel, rope=rope, n_q=n_q, n_norm=n_norm, v_add=v_add is not None),
        grid=(b, l // tile),
        in_specs=in_specs,
        out_specs=[pl.BlockSpec((1, tile, n_q), row), pl.BlockSpec((1, tile, n_k), row),
                   pl.BlockSpec((1, tile, n_v), row)],
        out_shape=[jax.ShapeDtypeStruct((b, l, n_q), BF16), jax.ShapeDtypeStruct((b, l, n_k), BF16),
                   jax.ShapeDtypeStruct((b, l, n_v), BF16)],
        compiler_params=_cparams(("arbitrary", "arbitrary")),
        name="qkv_proj",
    )(*args)


def _dot_nt(a, b):
    return lax.dot_general(a, b, (((1,), (1,)), ((), ())), preferred_element_type=F32)


def _row_max(s_list):
    m = s_list[0].max(axis=-1, keepdims=True)
    for s in s_list[1:]:
        m = jnp.maximum(m, s.max(axis=-1, keepdims=True))
    return m


def _split_heads(qp):
    mask_a = _head_a_mask()
    zero = jnp.zeros_like(qp)
    return jnp.where(mask_a, qp, zero), jnp.where(mask_a, zero, qp)


def _softmax_pv(s_list, v_list, sink_logit, dv):
    m = _row_max(s_list)
    if sink_logit is not None:
        m = jnp.maximum(m, sink_logit)
    o = None
    for s, v in zip(s_list, v_list):
        oj = jnp.dot(jnp.exp2(s - m).astype(BF16), v, preferred_element_type=F32)
        o = oj if o is None else o + oj
    den = o[:, dv:dv + 1]
    if sink_logit is not None:
        den = den + jnp.exp2(sink_logit - m)
    return o[:, :dv] * (1.0 / den)


def _head_tiles(q_ref, tq, per_trip, tile_ctx, logits, finish, emit):
    nq = q_ref.shape[1] // tq
    assert nq % per_trip == 0

    def heads(r0):
        hs = []
        for p0 in range(0, q_ref.shape[2], LANES):
            hs += list(_split_heads(q_ref[0, pl.ds(r0, tq), p0:p0 + LANES]))
        return hs

    def body(i, carry):
        pending = None
        outs = []
        r_prev = None
        for u in range(per_trip):
            r0 = pl.multiple_of((i * per_trip + u) * tq, tq)
            ctx = tile_ctx(r0)
            for g, q in enumerate(heads(r0)):
                s = logits(ctx, q)
                if pending is not None:
                    outs.append(finish(*pending))
                pending = (ctx, s, g)
                if g == 0 and r_prev is not None:
                    emit(r_prev, outs)
                    outs = []
            r_prev = r0
        outs.append(finish(*pending))
        emit(r_prev, outs)
        return carry

    lax.fori_loop(0, nq // per_trip, body, 0)


def _diff_attn_kernel(*refs, lam_init, n_src, tq, per_trip):
    lam_ref, hg_ref, q_ref = refs[:3]
    kv_refs = refs[3:3 + 2 * n_src]
    o_ref = refs[3 + 2 * n_src]
    va_refs = refs[4 + 2 * n_src:]
    lq = lam_ref[...]
    lam = (jnp.exp(jnp.sum(lq[0:1] * lq[1:2], axis=-1, keepdims=True))
           - jnp.exp(jnp.sum(lq[2:3] * lq[3:4], axis=-1, keepdims=True)) + lam_init)
    hg = hg_ref[...] * (1.0 - lam_init)
    lane = lax.broadcasted_iota(jnp.int32, (1, LANES), 1)
    for j in range(n_src):
        s = kv_refs[2 * j].shape[1]
        va_refs[j][:, :DIFF_V_DIM] = kv_refs[2 * j + 1][0]
        va_refs[j][:, DIFF_V_DIM:] = jnp.broadcast_to(
            jnp.where(lane == 0, 1.0, 0.0).astype(BF16), (s, LANES))

    def logits(ctx, q):
        return [_dot_nt(q, kv_refs[2 * j][0]) for j in range(n_src)]

    def finish(ctx, s, g):
        return _softmax_pv(s, [va_refs[j][...] for j in range(n_src)], None, DIFF_V_DIM)

    def emit(r0, outs):
        o = outs[0] - lam * outs[1]
        o_ref[0, pl.ds(r0, tq), :] = (_rms(o) * hg).astype(BF16)

    _head_tiles(q_ref, tq, per_trip, lambda r0: None, logits, finish, emit)


def _diff_attention(lam_p, head_g, q, kvs, *, lam_init, tq, per_trip):
    b, lq, d = q.shape
    n_src = len(kvs)
    blk = lambda bi, h: (bi, 0, h)
    in_specs = [pl.BlockSpec((4, HEAD_DIM), lambda bi, h: (0, 0)),
                pl.BlockSpec((1, DIFF_V_DIM), lambda bi, h: (0, 0)),
                pl.BlockSpec((1, lq, LANES), blk)]
    args = [lam_p, head_g, q]
    for k, v in kvs:
        in_specs += [pl.BlockSpec((1, k.shape[1], LANES), blk)] * 2
        args += [k, v]
    return pl.pallas_call(
        functools.partial(_diff_attn_kernel, lam_init=lam_init, n_src=n_src, tq=tq,
                          per_trip=per_trip),
        grid=(b, DIFF_HEADS),
        in_specs=in_specs,
        out_specs=pl.BlockSpec((1, lq, LANES), blk),
        out_shape=jax.ShapeDtypeStruct((b, lq, d), BF16),
        scratch_shapes=[pltpu.VMEM((k.shape[1], DIFF_V_DIM + LANES), BF16) for k, _ in kvs],
        compiler_params=_cparams(("arbitrary", "arbitrary")),
        name="diff_attn",
    )(*args)


def _gqa_emit(o_ref, tq):
    def emit(r0, outs):
        o_ref[0, pl.ds(r0, tq), :] = jnp.concatenate(outs, axis=-1).astype(BF16)
    return emit


def _gqa_dense_kernel(*refs, n_src, tq, per_trip, sink):
    if sink:
        sink_ref = refs[0]
        refs = refs[1:]
    q_ref = refs[0]
    kv_refs = refs[1:1 + 2 * n_src]
    o_ref = refs[1 + 2 * n_src]
    kk = pl.program_id(1)

    def logits(ctx, q):
        return [_dot_nt(q, kv_refs[2 * j][0]) for j in range(n_src)]

    def finish(ctx, s, g):
        v = [kv_refs[2 * j + 1][0] for j in range(n_src)]
        return _softmax_pv(s, v, sink_ref[kk, g] * LOG2E if sink else None, HEAD_DIM)

    _head_tiles(q_ref, tq, per_trip, lambda r0: None, logits, finish, _gqa_emit(o_ref, tq))


def _gqa_dense_attention(q, kvs, *, tq, per_trip, sink=None):
    b, lq, d = q.shape
    n_src = len(kvs)
    in_specs = []
    args = []
    if sink is not None:
        in_specs.append(pl.BlockSpec(memory_space=pltpu.SMEM))
        args.append(sink)
    blk = lambda bi, kk: (bi, 0, kk)
    in_specs.append(pl.BlockSpec((1, lq, GQA_GROUP * HEAD_DIM), blk))
    args.append(q)
    for k, v in kvs:
        in_specs += [pl.BlockSpec((1, k.shape[1], LANES), blk)] * 2
        args += [k, v]
    return pl.pallas_call(
        functools.partial(_gqa_dense_kernel, n_src=n_src, tq=tq, per_trip=per_trip,
                          sink=sink is not None),
        grid=(b, GQA_KV_HEADS),
        in_specs=in_specs,
        out_specs=pl.BlockSpec((1, lq, GQA_GROUP * HEAD_DIM), blk),
        out_shape=jax.ShapeDtypeStruct((b, lq, d), BF16),
        compiler_params=_cparams(("arbitrary", "arbitrary")),
        name="gqa_dense_attn",
    )(*args)


def _gqa_window_kernel(sink_ref, q_ref, kl_ref, vl_ref, kc_ref, vc_ref, o_ref, *, tq, per_trip):
    n_lat = q_ref.shape[1]
    band = min(tq + 2 * WINDOW, n_lat)
    kk = pl.program_id(1)
    rel0 = (lax.broadcasted_iota(jnp.int32, (tq, band), 1)
            - lax.broadcasted_iota(jnp.int32, (tq, band), 0))

    def tile_ctx(r0):
        start = pl.multiple_of(jnp.clip(r0 - WINDOW, 0, n_lat - band), WINDOW)
        kb = kl_ref[0, pl.ds(start, band), :]
        vb = vl_ref[0, pl.ds(start, band), :]
        in_band = jnp.abs(rel0 + (start - r0)) <= WINDOW
        return kb, vb, in_band

    def logits(ctx, q):
        kb, _, in_band = ctx
        return [jnp.where(in_band, _dot_nt(q, kb), NEG_BIG), _dot_nt(q, kc_ref[0])]

    def finish(ctx, s, g):
        return _softmax_pv(s, [ctx[1], vc_ref[0]], sink_ref[kk, g] * LOG2E, HEAD_DIM)

    _head_tiles(q_ref, tq, per_trip, tile_ctx, logits, finish, _gqa_emit(o_ref, tq))


def _gqa_window_attention(q, k_l, v_l, k_c, v_c, sink, *, tq, per_trip):
    b, lq, d = q.shape
    lc = k_c.shape[1]
    blk = lambda bi, kk: (bi, 0, kk)
    return pl.pallas_call(
        functools.partial(_gqa_window_kernel, tq=tq, per_trip=per_trip),
        grid=(b, GQA_KV_HEADS),
        in_specs=[pl.BlockSpec(memory_space=pltpu.SMEM),
                  pl.BlockSpec((1, lq, GQA_GROUP * HEAD_DIM), blk),
                  pl.BlockSpec((1, lq, LANES), blk),
                  pl.BlockSpec((1, lq, LANES), blk),
                  pl.BlockSpec((1, lc, LANES), blk),
                  pl.BlockSpec((1, lc, LANES), blk)],
        out_specs=pl.BlockSpec((1, lq, GQA_GROUP * HEAD_DIM), blk),
        out_shape=jax.ShapeDtypeStruct((b, lq, d), BF16),
        compiler_params=_cparams(("arbitrary", "arbitrary")),
        name="gqa_window_attn",
    )(sink, q, k_l, v_l, k_c, v_c)


def _tail_kernel(op_ref, o_ref, ox_ref, xp_ref, x_ref, xx_ref, mod_ref, g_ref, wo_ref, wu_ref,
                 cw_ref, cb_ref, wd_ref, out_ref, olhs_ref, xn_ref, lhs_ref, *, tile, sub,
                 tiles_per_seq):
    ti = pl.program_id(1)
    oh = BF16_SUBLANES
    xh = SUBLANES
    m = tile + 2 * xh
    olhs_ref[0:oh, :] = op_ref[0]
    olhs_ref[oh:oh + tile, :] = o_ref[0]
    olhs_ref[oh + tile:, :] = ox_ref[0]
    y = jnp.dot(olhs_ref[...], wo_ref[0], preferred_element_type=F32)[oh - xh:oh + tile + xh]
    g1 = mod_ref[0, 2:3, :]
    sh2 = mod_ref[0, 3:4, :]
    sc2 = mod_ref[0, 4:5, :]
    g2 = mod_ref[0, 5:6, :]
    xn = jnp.concatenate([xp_ref[0], x_ref[0], xx_ref[0]], axis=0) + g1 * y
    xn_ref[...] = xn
    hn = (_rms(xn) * g_ref[...]) * (1.0 + sc2) + sh2
    row = lax.broadcasted_iota(jnp.int32, (m, 1), 0)
    keep = jnp.logical_and(jnp.logical_or(ti > 0, row >= xh),
                           jnp.logical_or(ti < tiles_per_seq - 1, row < xh + tile))
    lhs_ref[...] = jnp.where(keep, hn, 0.0).astype(BF16)

    ms = sub + 2 * xh
    n_sub = tile // sub

    def conv(u, c0):
        prev = pltpu.roll(u, 1, 0)[xh:xh + sub]
        nxt = pltpu.roll(u, ms - 1, 0)[xh:xh + sub]
        cur = u[xh:xh + sub]
        return (prev * cw_ref[0, 0:1, c0:c0 + FF_CHUNK] + cur * cw_ref[0, 1:2, c0:c0 + FF_CHUNK]
                + nxt * cw_ref[0, 2:3, c0:c0 + FF_CHUNK] + cb_ref[0, :, c0:c0 + FF_CHUNK])

    def up(s, j):
        lhs = lhs_ref[s * sub:s * sub + ms, :]
        cg = j * FF_CHUNK
        cv = D_FF + j * FF_CHUNK
        return (jnp.dot(lhs, wu_ref[0, :, cg:cg + FF_CHUNK], preferred_element_type=F32),
                jnp.dot(lhs, wu_ref[0, :, cv:cv + FF_CHUNK], preferred_element_type=F32))

    def gated(j, u):
        cg = j * FF_CHUNK
        gate = conv(u[0], cg)
        val = conv(u[1], D_FF + cg)
        return (gate * jax.nn.sigmoid(gate) * val).astype(BF16)

    n_chunks = D_FF // FF_CHUNK
    accs = [None] * n_sub
    us, acts = {}, {}
    for step in range(n_chunks + 2):
        for s in range(n_sub):
            if step < n_chunks:
                us[s, step] = up(s, step)
            if 0 <= step - 1 < n_chunks:
                acts[s, step - 1] = gated(step - 1, us.pop((s, step - 1)))
            if 0 <= step - 2 < n_chunks:
                cg = (step - 2) * FF_CHUNK
                part = jnp.dot(acts.pop((s, step - 2)), wd_ref[0, cg:cg + FF_CHUNK, :],
                               preferred_element_type=F32)
                accs[s] = part if accs[s] is None else accs[s] + part
    for s in range(n_sub):
        out_ref[0, s * sub:(s + 1) * sub, :] = (xn_ref[xh + s * sub:xh + (s + 1) * sub, :]
                                                + g2 * accs[s])


def _tail(o, x, mod, g2n, w_o, wo_layer, w_up, conv_w, conv_b, w_down, layer, *, tile, alias):
    b, l, d = x.shape
    nmod = mod.shape[0]
    nt = l // tile
    ob = tile // BF16_SUBLANES
    xb = tile // SUBLANES
    mod_map = (lambda bi, ti: (bi, 0, 0)) if nmod > 1 else (lambda bi, ti: (0, 0, 0))
    row = lambda bi, ti: (bi, ti, 0)
    lay = lambda bi, ti: (layer, 0, 0)
    lay_o = lambda bi, ti: (wo_layer, 0, 0)
    prev = lambda nb: (lambda bi, ti: (bi, jnp.maximum(ti * nb - 1, 0), 0))
    nxt = lambda nb: (lambda bi, ti: (bi, jnp.minimum((ti + 1) * nb, nt * nb - 1), 0))
    return pl.pallas_call(
        functools.partial(_tail_kernel, tile=tile, sub=min(TAIL_SUB, tile), tiles_per_seq=nt),
        grid=(b, nt),
        in_specs=[pl.BlockSpec((1, BF16_SUBLANES, d), prev(ob)),
                  pl.BlockSpec((1, tile, d), row),
                  pl.BlockSpec((1, BF16_SUBLANES, d), nxt(ob)),
                  pl.BlockSpec((1, SUBLANES, d), prev(xb)),
                  pl.BlockSpec((1, tile, d), row),
                  pl.BlockSpec((1, SUBLANES, d), nxt(xb)),
                  pl.BlockSpec((1, MOD_ROWS, d), mod_map),
                  pl.BlockSpec((1, d), lambda bi, ti: (0, 0)),
                  _resident((1, d, d), lay_o),
                  _resident((1, d, 2 * D_FF), lay),
                  pl.BlockSpec((1, CONV_W, 2 * D_FF), lay),
                  pl.BlockSpec((1, 1, 2 * D_FF), lay),
                  _resident((1, D_FF, d), lay)],
        out_specs=pl.BlockSpec((1, tile, d), row),
        out_shape=jax.ShapeDtypeStruct((b, l, d), F32),
        scratch_shapes=[pltpu.VMEM((tile + 2 * BF16_SUBLANES, d), BF16),
                        pltpu.VMEM((tile + 2 * SUBLANES, d), F32),
                        pltpu.VMEM((tile + 2 * SUBLANES, d), BF16)],
        input_output_aliases={4: 0} if alias else {},
        compiler_params=_cparams(("arbitrary", "arbitrary")),
        name="layer_tail",
    )(o, o, o, x, x, x, mod, g2n, w_o, w_up, conv_w, conv_b, w_down)


def _rope_tables(n_lat):
    t = jnp.arange(n_lat)
    row = (t // GRID_W).astype(F32)
    col = (t % GRID_W).astype(F32)
    n_freq = HEAD_DIM // 4
    inv_freq = ROPE_THETA ** (-jnp.arange(n_freq, dtype=F32) / n_freq)
    ang = jnp.concatenate([row[:, None] * inv_freq, col[:, None] * inv_freq], axis=-1)
    cos = jnp.cos(ang)
    sin = jnp.sin(ang)
    return (jnp.concatenate([cos, cos, cos, cos], axis=-1),
            jnp.concatenate([-sin, -sin, sin, sin], axis=-1))


def _pair_layout(w):
    lead = w.shape[:-1]
    w = w.reshape(lead + (-1, 2, 2, QUARTER))
    return jnp.swapaxes(w, -3, -2).reshape(lead + (-1,))


def _dup_layout(w):
    lead = w.shape[:-1]
    w = w.reshape(lead + (-1, 2, 1, QUARTER))
    return jnp.broadcast_to(w, w.shape[:-2] + (2, QUARTER)).reshape(lead + (-1,))


def _head_mean_matrix():
    lane = np.arange(MXU_W)
    head = (lane // LANES) * 2 + (lane // QUARTER) % 2
    return jnp.asarray((head[:, None] == head[None, :]).astype(np.float32) / HEAD_DIM, dtype=BF16)


def _prep_qkv(w_qkv, qk_g, diff):
    n_q = D_MODEL
    w_q = _pair_layout(w_qkv[:, :n_q])
    if diff:
        n_kv = D_MODEL
        w_k = _pair_layout(w_qkv[:, n_q:n_q + n_kv])
        w_v = w_qkv[:, n_q + n_kv:]
        v_add = None
    else:
        n_kv = GQA_KV_HEADS * HEAD_DIM
        w_k = _dup_layout(w_qkv[:, n_q:n_q + n_kv])
        w_v = w_qkv[:, n_q + n_kv:].reshape(D_MODEL, GQA_KV_HEADS, HEAD_DIM)
        w_v = jnp.pad(w_v, ((0, 0), (0, 0), (0, LANES - HEAD_DIM))).reshape(D_MODEL, -1)
        one_col = np.zeros((LANES,), np.float32)
        one_col[HEAD_DIM] = 1.0
        v_add = jnp.asarray(np.tile(one_col, GQA_KV_HEADS)).reshape(1, -1)
    w = jnp.concatenate([w_q, w_k, w_v], axis=1)
    gain = jnp.concatenate([jnp.tile(_dup_layout(qk_g[0] * Q_SCALE), n_q // LANES),
                            jnp.tile(_dup_layout(qk_g[1]), w_k.shape[1] // LANES)])
    return w, gain.reshape(1, -1), v_add, n_q, n_q + w_k.shape[1]


def kernel(x, c, ctx, c_ctx, adaln_w, adaln_b, norm1_g, norm2_g, ffn_w_up, ffn_conv_w, ffn_conv_b, ffn_w_down, a_w_qkv, a_qk_g, a_lambda, a_head_g, a_w_o, b_w_qkv, b_qk_g, b_w_o, c_w_qkv, c_qk_g, c_sink, c_w_o):
    b, n_lat, d = x.shape
    n_ctx = ctx.shape[1]
    depth = adaln_w.shape[0]
    assert d == D_MODEL and b + 1 <= C_ROWS

    tabs = _rope_tables(n_lat)
    bd = _head_mean_matrix()

    cvec = jnp.concatenate([c, c_ctx[None, :], jnp.zeros((C_ROWS - b - 1, d), F32)], axis=0)
    mod_all = _adaln(cvec, adaln_w, adaln_b).reshape(depth, C_ROWS, 6, d)
    mod_all = jnp.pad(mod_all, ((0, 0), (0, 0), (0, MOD_ROWS - 6), (0, 0)))

    w_up_all = _to_bf16(ffn_w_up)
    w_down_all = _to_bf16(ffn_w_down)
    conv_b_all = ffn_conv_b.reshape(depth, 1, 2 * D_FF)
    qkv_all = [_to_bf16(w) for w in (a_w_qkv, b_w_qkv, c_w_qkv)]
    wo_all = [_to_bf16(w) for w in (a_w_o, b_w_o, c_w_o)]
    qk_g_all = (a_qk_g, b_qk_g, c_qk_g)

    qkv_tile = min(512, n_lat)
    tail_tile = min(512, n_lat)
    tq_lat = min(256, n_lat)
    tq_ctx = min(256, n_ctx)
    trip = lambda lq, tq, want: max(p for p in (1, 2, 4, 8) if p <= want and (lq // tq) % p == 0)

    h_ctx = ctx
    for i in range(depth):
        last = i == depth - 1
        j = i // N_MIXERS
        kind = i % N_MIXERS
        mod_l = mod_all[i, :b]
        mod_c = mod_all[i, b:b + 1]
        g1n = norm1_g[i].reshape(1, d)
        g2n = norm2_g[i].reshape(1, d)
        w_all, gain, v_add, n_q, n_norm = _prep_qkv(qkv_all[kind][j], qk_g_all[kind][j], kind == 0)

        q_l, k_l, v_l = _qkv_proj(x, mod_l, g1n, w_all, gain, bd, tabs, v_add,
                                  n_q=n_q, n_norm=n_norm, tile=qkv_tile)
        q_c, k_c, v_c = _qkv_proj(h_ctx, mod_c, g1n, w_all, gain, bd, None, v_add,
                                  n_q=n_q, n_norm=n_norm, tile=n_ctx)

        o_c = None
        if kind == 0:
            lam_init = 0.8 - 0.6 * math.exp(-0.3 * i)
            hg = a_head_g[j].reshape(1, DIFF_V_DIM)
            o_l = _diff_attention(a_lambda[j], hg, q_l, [(k_l, v_l), (k_c, v_c)], lam_init=lam_init,
                                  tq=tq_lat, per_trip=trip(n_lat, tq_lat, 8))
            if not last:
                o_c = _diff_attention(a_lambda[j], hg, q_c, [(k_c, v_c)], lam_init=lam_init,
                                      tq=tq_ctx // 2, per_trip=2)
        elif kind == 1:
            o_l = _gqa_dense_attention(q_l, [(k_l, v_l), (k_c, v_c)], tq=tq_lat,
                                       per_trip=trip(n_lat, tq_lat, 4))
            if not last:
                o_c = _gqa_dense_attention(q_c, [(k_c, v_c)], tq=tq_ctx, per_trip=1)
        else:
            sink = c_sink[j].reshape(GQA_KV_HEADS, GQA_GROUP)
            o_l = _gqa_window_attention(q_l, k_l, v_l, k_c, v_c, sink, tq=tq_lat,
                                        per_trip=trip(n_lat, tq_lat, 4))
            if not last:
                o_c = _gqa_dense_attention(q_c, [(k_c, v_c)], tq=tq_ctx, per_trip=1, sink=sink)

        tail = functools.partial(_tail, g2n=g2n, w_o=wo_all[kind], wo_layer=j, w_up=w_up_all,
                                 conv_w=ffn_conv_w, conv_b=conv_b_all, w_down=w_down_all, layer=i,
                                 alias=False)
        x = tail(o_l, x, mod_l, tile=tail_tile)
        if not last:
            h_ctx = tail(o_c, h_ctx, mod_c, tile=n_ctx)
    return x
```

```python
import functools
import math

import numpy as np
import jax
import jax.numpy as jnp
from jax import lax
from jax.experimental import pallas as pl
from jax.experimental.pallas import tpu as pltpu

F32 = jnp.float32
BF16 = jnp.bfloat16

D_MODEL = 1024
HEAD_DIM = 64
GRID_W = 64
DIFF_HEADS = D_MODEL // (2 * HEAD_DIM)
DIFF_V_DIM = 2 * HEAD_DIM
GQA_HEADS = D_MODEL // HEAD_DIM
GQA_KV_HEADS = GQA_HEADS // 4
GQA_GROUP = GQA_HEADS // GQA_KV_HEADS
WINDOW = 128
D_FF = 256 * ((8 * D_MODEL // 3 + 255) // 256)
CONV_W = 3
ROPE_THETA = 10000.0
EPS = 1e-6
ATTN_SCALE = HEAD_DIM ** -0.5
LOG2E = 1.0 / math.log(2.0)
Q_SCALE = ATTN_SCALE * LOG2E
N_MIXERS = 3

LANES = 128
SUBLANES = 8
QUARTER = HEAD_DIM // 2
BF16_SUBLANES = 16
MXU_W = 256
FF_CHUNK = MXU_W
TAIL_SUB = 256
MOD_ROWS = 8
C_ROWS = 24
VMEM_LIMIT = 56 * 1024 * 1024
CAST_BLOCK_BYTES = 4 * 1024 * 1024
NEG_BIG = -1e30


def _cparams(sem, vmem=VMEM_LIMIT):
    return pltpu.CompilerParams(dimension_semantics=sem, vmem_limit_bytes=vmem)


def _resident(shape, index_map):
    return pl.BlockSpec(shape, index_map, pipeline_mode=pl.Buffered(1))


def _rms(x):
    return x * lax.rsqrt(jnp.mean(x * x, axis=-1, keepdims=True) + EPS)


def _head_a_mask():
    lane = lax.broadcasted_iota(jnp.int32, (1, LANES), 1)
    return (lane & QUARTER) == 0


def _cast_kernel(x_ref, o_ref):
    o_ref[...] = x_ref[...].astype(BF16)


def _to_bf16(w):
    n, r, c = w.shape
    parts = 1
    while (r // parts) * c * 4 > CAST_BLOCK_BYTES or r % parts or (r // parts) % BF16_SUBLANES:
        parts += 1
    tr = r // parts
    blk = lambda i, j: (i, j, 0)
    return pl.pallas_call(
        _cast_kernel,
        grid=(n, parts),
        in_specs=[pl.BlockSpec((1, tr, c), blk)],
        out_specs=pl.BlockSpec((1, tr, c), blk),
        out_shape=jax.ShapeDtypeStruct(w.shape, BF16),
        compiler_params=_cparams(("arbitrary", "arbitrary"), 32 * 1024 * 1024),
        name="cast_bf16",
    )(w)


def _adaln_kernel(c_ref, w_ref, b_ref, o_ref):
    c = c_ref[...]
    s = c * jax.nn.sigmoid(c)
    o_ref[0] = jnp.dot(s.astype(BF16), w_ref[0].astype(BF16),
                       preferred_element_type=F32) + b_ref[0]


def _adaln(cvec, adaln_w, adaln_b):
    depth, d, n = adaln_w.shape
    tn = 1536
    return pl.pallas_call(
        _adaln_kernel,
        grid=(depth, n // tn),
        in_specs=[pl.BlockSpec((C_ROWS, d), lambda i, j: (0, 0)),
                  pl.BlockSpec((1, d, tn), lambda i, j: (i, 0, j)),
                  pl.BlockSpec((1, 1, tn), lambda i, j: (i, 0, j))],
        out_specs=pl.BlockSpec((1, C_ROWS, tn), lambda i, j: (i, 0, j)),
        out_shape=jax.ShapeDtypeStruct((depth, C_ROWS, n), F32),
        compiler_params=_cparams(("arbitrary", "arbitrary"), 40 * 1024 * 1024),
        name="adaln",
    )(cvec, adaln_w, adaln_b.reshape(depth, 1, n))


def _qkv_kernel(*refs, rope, n_q, n_norm, v_add):
    x_ref, mod_ref, g_ref, w_ref, gain_ref, bd_ref = refs[:6]
    pos = 6
    if rope:
        cos = refs[pos][...]
        sin = refs[pos + 1][...]
        pos += 2
    if v_add:
        va_ref = refs[pos]
        pos += 1
    q_ref, k_ref, v_ref = refs[pos:]
    n_total = w_ref.shape[1]

    x = x_ref[0]
    sh = mod_ref[0, 0:1, :]
    sc = mod_ref[0, 1:2, :]
    hn = ((_rms(x) * g_ref[...]) * (1.0 + sc) + sh).astype(BF16)

    def project(c0):
        return jnp.dot(hn, w_ref[:, c0:c0 + MXU_W], preferred_element_type=F32)

    def mean_sq(z):
        return jnp.dot((z * z).astype(BF16), bd_ref[...], preferred_element_type=F32)

    def finish(c0, z, msq):
        if c0 < n_norm:
            z = z * lax.rsqrt(msq + EPS) * gain_ref[:, c0:c0 + MXU_W]
            if rope:
                z = jnp.concatenate(
                    [z[:, s0:s0 + LANES] * cos + pltpu.roll(z[:, s0:s0 + LANES], HEAD_DIM, 1) * sin
                     for s0 in range(0, MXU_W, LANES)], axis=-1)
            if c0 < n_q:
                q_ref[0, :, c0:c0 + MXU_W] = z.astype(BF16)
            else:
                k_ref[0, :, c0 - n_q:c0 - n_q + MXU_W] = z.astype(BF16)
        else:
            if v_add:
                z = z + va_ref[:, c0 - n_norm:c0 - n_norm + MXU_W]
            v_ref[0, :, c0 - n_norm:c0 - n_norm + MXU_W] = z.astype(BF16)

    starts = list(range(0, n_total, MXU_W))
    zs, ms = {}, {}
    for step in range(len(starts) + 2):
        if step < len(starts):
            zs[step] = project(starts[step])
        if 0 <= step - 1 < len(starts) and starts[step - 1] < n_norm:
            ms[step - 1] = mean_sq(zs[step - 1])
        if 0 <= step - 2 < len(starts):
            finish(starts[step - 2], zs.pop(step - 2), ms.pop(step - 2, None))


def _qkv_proj(x, mod, g, w, gain, bd, tabs, v_add, *, n_q, n_norm, tile):
    b, l, d = x.shape
    nmod = mod.shape[0]
    n = w.shape[1]
    n_k = n_norm - n_q
    n_v = n - n_norm
    rope = tabs is not None
    mod_map = (lambda bi, ti: (bi, 0, 0)) if nmod > 1 else (lambda bi, ti: (0, 0, 0))
    const2 = lambda bi, ti: (0, 0)
    row = lambda bi, ti: (bi, ti, 0)
    in_specs = [pl.BlockSpec((1, tile, d), row),
                pl.BlockSpec((1, MOD_ROWS, d), mod_map),
                pl.BlockSpec((1, d), const2),
                _resident((d, n), const2),
                pl.BlockSpec((1, n_norm), const2),
                pl.BlockSpec((MXU_W, MXU_W), const2)]
    args = [x, mod, g, w, gain, bd]
    if rope:
        in_specs += [pl.BlockSpec((tile, LANES), lambda bi, ti: (ti, 0))] * 2
        args += list(tabs)
    if v_add is not None:
        in_specs.append(pl.BlockSpec((1, n_v), const2))
        args.append(v_add)
    return pl.pallas_call(
        functools.partial(_qkv_kernel, rope=rope, n_q=n_q, n_norm=n_norm, v_add=v_add is not None),
        grid=(b, l // tile),
        in_specs=in_specs,
        out_specs=[pl.BlockSpec((1, tile, n_q), row), pl.BlockSpec((1, tile, n_k), row),
                   pl.BlockSpec((1, tile, n_v), row)],
        out_shape=[jax.ShapeDtypeStruct((b, l, n_q), BF16), jax.ShapeDtypeStruct((b, l, n_k), BF16),
                   jax.ShapeDtypeStruct((b, l, n_v), BF16)],
        compiler_params=_cparams(("arbitrary", "arbitrary")),
        name="qkv_proj",
    )(*args)


def _dot_nt(a, b):
    return lax.dot_general(a, b, (((1,), (1,)), ((), ())), preferred_element_type=F32)


def _row_max(s_list):
    m = s_list[0].max(axis=-1, keepdims=True)
    for s in s_list[1:]:
        m = jnp.maximum(m, s.max(axis=-1, keepdims=True))
    return m


def _split_heads(qp):
    mask_a = _head_a_mask()
    zero = jnp.zeros_like(qp)
    return jnp.where(mask_a, qp, zero), jnp.where(mask_a, zero, qp)


def _softmax_pv(s_list, v_list, sink_logit, dv):
    m = _row_max(s_list)
    if sink_logit is not None:
        m = jnp.maximum(m, sink_logit)
    o = None
    for s, v in zip(s_list, v_list):
        oj = jnp.dot(jnp.exp2(s - m).astype(BF16), v, preferred_element_type=F32)
        o = oj if o is None else o + oj
    den = o[:, dv:dv + 1]
    if sink_logit is not None:
        den = den + jnp.exp2(sink_logit - m)
    return o[:, :dv] * (1.0 / den)


def _head_tiles(q_ref, tq, per_trip, tile_ctx, logits, finish, emit, lookahead=1):
    nq = q_ref.shape[1] // tq
    assert nq % per_trip == 0
    n_heads = 2 * (q_ref.shape[2] // LANES)

    def heads(r0):
        hs = []
        for p0 in range(0, q_ref.shape[2], LANES):
            hs += list(_split_heads(q_ref[0, pl.ds(r0, tq), p0:p0 + LANES]))
        return hs

    def body(i, carry):
        pending = []
        outs = []

        def retire():
            r_done, ctx, s, g = pending.pop(0)
            outs.append(finish(ctx, s, g))
            if len(outs) == n_heads:
                emit(r_done, list(outs))
                outs.clear()

        for u in range(per_trip):
            r0 = pl.multiple_of((i * per_trip + u) * tq, tq)
            ctx = tile_ctx(r0)
            for g, q in enumerate(heads(r0)):
                pending.append((r0, ctx, logits(ctx, q, g), g))
                if len(pending) > lookahead:
                    retire()
        while pending:
            retire()
        return carry

    lax.fori_loop(0, nq // per_trip, body, 0)


def _diff_attn_kernel(*refs, lam_init, n_src, tq, per_trip):
    lam_ref, hg_ref, q_ref = refs[:3]
    kv_refs = refs[3:3 + 2 * n_src]
    o_ref = refs[3 + 2 * n_src]
    va_refs = refs[4 + 2 * n_src:]
    n_heads = q_ref.shape[2] // LANES
    lq = lam_ref[...]
    lam = (jnp.exp(jnp.sum(lq[0:1] * lq[1:2], axis=-1, keepdims=True))
           - jnp.exp(jnp.sum(lq[2:3] * lq[3:4], axis=-1, keepdims=True)) + lam_init)
    hg = hg_ref[...] * (1.0 - lam_init)
    lane = lax.broadcasted_iota(jnp.int32, (1, LANES), 1)
    for h in range(n_heads):
        for j in range(n_src):
            s = kv_refs[2 * j].shape[1]
            va = va_refs[h * n_src + j]
            va[:, :DIFF_V_DIM] = kv_refs[2 * j + 1][0, :, h * LANES:(h + 1) * LANES]
            va[:, DIFF_V_DIM:] = jnp.broadcast_to(
                jnp.where(lane == 0, 1.0, 0.0).astype(BF16), (s, LANES))

    def logits(ctx, q, g):
        h = g // 2
        return [_dot_nt(q, kv_refs[2 * j][0, :, h * LANES:(h + 1) * LANES]) for j in range(n_src)]

    def finish(ctx, s, g):
        h = g // 2
        return _softmax_pv(s, [va_refs[h * n_src + j][...] for j in range(n_src)], None,
                           DIFF_V_DIM)

    def emit(r0, outs):
        for h in range(n_heads):
            o = outs[2 * h] - lam * outs[2 * h + 1]
            o_ref[0, pl.ds(r0, tq), h * LANES:(h + 1) * LANES] = (_rms(o) * hg).astype(BF16)

    _head_tiles(q_ref, tq, per_trip, lambda r0: None, logits, finish, emit)


def _diff_attention(lam_p, head_g, q, kvs, *, lam_init, tq, per_trip, heads_per_step):
    b, lq, d = q.shape
    n_src = len(kvs)
    width = heads_per_step * LANES
    blk = lambda bi, h: (bi, 0, h)
    in_specs = [pl.BlockSpec((4, HEAD_DIM), lambda bi, h: (0, 0)),
                pl.BlockSpec((1, DIFF_V_DIM), lambda bi, h: (0, 0)),
                pl.BlockSpec((1, lq, width), blk)]
    args = [lam_p, head_g, q]
    for k, v in kvs:
        in_specs += [pl.BlockSpec((1, k.shape[1], width), blk)] * 2
        args += [k, v]
    return pl.pallas_call(
        functools.partial(_diff_attn_kernel, lam_init=lam_init, n_src=n_src, tq=tq,
                          per_trip=per_trip),
        grid=(b, DIFF_HEADS // heads_per_step),
        in_specs=in_specs,
        out_specs=pl.BlockSpec((1, lq, width), blk),
        out_shape=jax.ShapeDtypeStruct((b, lq, d), BF16),
        scratch_shapes=[pltpu.VMEM((k.shape[1], DIFF_V_DIM + LANES), BF16)
                        for _ in range(heads_per_step) for k, _ in kvs],
        compiler_params=_cparams(("arbitrary", "arbitrary")),
        name="diff_attn",
    )(*args)


def _gqa_emit(o_ref, tq):
    def emit(r0, outs):
        o_ref[0, pl.ds(r0, tq), :] = jnp.concatenate(outs, axis=-1).astype(BF16)
    return emit


def _gqa_dense_kernel(*refs, n_src, tq, per_trip, sink):
    if sink:
        sink_ref = refs[0]
        refs = refs[1:]
    q_ref = refs[0]
    kv_refs = refs[1:1 + 2 * n_src]
    o_ref = refs[1 + 2 * n_src]
    groups = q_ref.shape[2] // (GQA_GROUP * HEAD_DIM)
    kk0 = pl.program_id(1) * groups

    def kv_lanes(g):
        kv = g // GQA_GROUP
        return slice(kv * LANES, (kv + 1) * LANES)

    def logits(ctx, q, g):
        return [_dot_nt(q, kv_refs[2 * j][0, :, kv_lanes(g)]) for j in range(n_src)]

    def finish(ctx, s, g):
        v = [kv_refs[2 * j + 1][0, :, kv_lanes(g)] for j in range(n_src)]
        sink_logit = sink_ref[kk0 + g // GQA_GROUP, g % GQA_GROUP] * LOG2E if sink else None
        return _softmax_pv(s, v, sink_logit, HEAD_DIM)

    _head_tiles(q_ref, tq, per_trip, lambda r0: None, logits, finish, _gqa_emit(o_ref, tq))


def _gqa_dense_attention(q, kvs, *, tq, per_trip, groups_per_step=1, sink=None):
    b, lq, d = q.shape
    n_src = len(kvs)
    in_specs = []
    args = []
    if sink is not None:
        in_specs.append(pl.BlockSpec(memory_space=pltpu.SMEM))
        args.append(sink)
    blk = lambda bi, kk: (bi, 0, kk)
    q_width = groups_per_step * GQA_GROUP * HEAD_DIM
    in_specs.append(pl.BlockSpec((1, lq, q_width), blk))
    args.append(q)
    for k, v in kvs:
        in_specs += [pl.BlockSpec((1, k.shape[1], groups_per_step * LANES), blk)] * 2
        args += [k, v]
    return pl.pallas_call(
        functools.partial(_gqa_dense_kernel, n_src=n_src, tq=tq, per_trip=per_trip,
                          sink=sink is not None),
        grid=(b, GQA_KV_HEADS // groups_per_step),
        in_specs=in_specs,
        out_specs=pl.BlockSpec((1, lq, q_width), blk),
        out_shape=jax.ShapeDtypeStruct((b, lq, d), BF16),
        compiler_params=_cparams(("arbitrary", "arbitrary")),
        name="gqa_dense_attn",
    )(*args)


def _gqa_window_kernel(sink_ref, q_ref, kl_ref, vl_ref, kc_ref, vc_ref, o_ref, *, tq, per_trip):
    n_lat = q_ref.shape[1]
    band = min(tq + 2 * WINDOW, n_lat)
    kk = pl.program_id(1)
    rel0 = (lax.broadcasted_iota(jnp.int32, (tq, band), 1)
            - lax.broadcasted_iota(jnp.int32, (tq, band), 0))

    def tile_ctx(r0):
        start = pl.multiple_of(jnp.clip(r0 - WINDOW, 0, n_lat - band), WINDOW)
        kb = kl_ref[0, pl.ds(start, band), :]
        vb = vl_ref[0, pl.ds(start, band), :]
        in_band = jnp.abs(rel0 + (start - r0)) <= WINDOW
        return kb, vb, in_band

    def logits(ctx, q, g):
        kb, _, in_band = ctx
        return [jnp.where(in_band, _dot_nt(q, kb), NEG_BIG), _dot_nt(q, kc_ref[0])]

    def finish(ctx, s, g):
        return _softmax_pv(s, [ctx[1], vc_ref[0]], sink_ref[kk, g] * LOG2E, HEAD_DIM)

    _head_tiles(q_ref, tq, per_trip, tile_ctx, logits, finish, _gqa_emit(o_ref, tq), lookahead=2)


def _gqa_window_attention(q, k_l, v_l, k_c, v_c, sink, *, tq, per_trip):
    b, lq, d = q.shape
    lc = k_c.shape[1]
    blk = lambda bi, kk: (bi, 0, kk)
    return pl.pallas_call(
        functools.partial(_gqa_window_kernel, tq=tq, per_trip=per_trip),
        grid=(b, GQA_KV_HEADS),
        in_specs=[pl.BlockSpec(memory_space=pltpu.SMEM),
                  pl.BlockSpec((1, lq, GQA_GROUP * HEAD_DIM), blk),
                  pl.BlockSpec((1, lq, LANES), blk),
                  pl.BlockSpec((1, lq, LANES), blk),
                  pl.BlockSpec((1, lc, LANES), blk),
                  pl.BlockSpec((1, lc, LANES), blk)],
        out_specs=pl.BlockSpec((1, lq, GQA_GROUP * HEAD_DIM), blk),
        out_shape=jax.ShapeDtypeStruct((b, lq, d), BF16),
        compiler_params=_cparams(("arbitrary", "arbitrary")),
        name="gqa_window_attn",
    )(sink, q, k_l, v_l, k_c, v_c)


def _tail_kernel(op_ref, o_ref, ox_ref, xp_ref, x_ref, xx_ref, mod_ref, g_ref, wo_ref, wu_ref,
                 cw_ref, cb_ref, wd_ref, out_ref, olhs_ref, xn_ref, lhs_ref, *, tile, sub,
                 tiles_per_seq):
    ti = pl.program_id(1)
    oh = BF16_SUBLANES
    xh = SUBLANES
    m = tile + 2 * xh
    olhs_ref[0:oh, :] = op_ref[0]
    olhs_ref[oh:oh + tile, :] = o_ref[0]
    olhs_ref[oh + tile:, :] = ox_ref[0]
    y = jnp.dot(olhs_ref[...], wo_ref[0], preferred_element_type=F32)[oh - xh:oh + tile + xh]
    g1 = mod_ref[0, 2:3, :]
    sh2 = mod_ref[0, 3:4, :]
    sc2 = mod_ref[0, 4:5, :]
    g2 = mod_ref[0, 5:6, :]
    xn = jnp.concatenate([xp_ref[0], x_ref[0], xx_ref[0]], axis=0) + g1 * y
    xn_ref[...] = xn
    hn = (_rms(xn) * g_ref[...]) * (1.0 + sc2) + sh2
    row = lax.broadcasted_iota(jnp.int32, (m, 1), 0)
    keep = jnp.logical_and(jnp.logical_or(ti > 0, row >= xh),
                           jnp.logical_or(ti < tiles_per_seq - 1, row < xh + tile))
    lhs_ref[...] = jnp.where(keep, hn, 0.0).astype(BF16)

    ms = sub + 2 * xh
    n_sub = tile // sub

    def conv(u, c0):
        prev = pltpu.roll(u, 1, 0)[xh:xh + sub]
        nxt = pltpu.roll(u, ms - 1, 0)[xh:xh + sub]
        cur = u[xh:xh + sub]
        return (prev * cw_ref[0, 0:1, c0:c0 + FF_CHUNK] + cur * cw_ref[0, 1:2, c0:c0 + FF_CHUNK]
                + nxt * cw_ref[0, 2:3, c0:c0 + FF_CHUNK] + cb_ref[0, :, c0:c0 + FF_CHUNK])

    def up(s, j):
        lhs = lhs_ref[s * sub:s * sub + ms, :]
        cg = j * FF_CHUNK
        cv = D_FF + j * FF_CHUNK
        return (jnp.dot(lhs, wu_ref[0, :, cg:cg + FF_CHUNK], preferred_element_type=F32),
                jnp.dot(lhs, wu_ref[0, :, cv:cv + FF_CHUNK], preferred_element_type=F32))

    def gated(j, u):
        cg = j * FF_CHUNK
        gate = conv(u[0], cg)
        val = conv(u[1], D_FF + cg)
        return (gate * jax.nn.sigmoid(gate) * val).astype(BF16)

    n_chunks = D_FF // FF_CHUNK
    accs = [None] * n_sub
    us, acts = {}, {}
    for step in range(n_chunks + 2):
        for s in range(n_sub):
            if step < n_chunks:
                us[s, step] = up(s, step)
            if 0 <= step - 1 < n_chunks:
                acts[s, step - 1] = gated(step - 1, us.pop((s, step - 1)))
            if 0 <= step - 2 < n_chunks:
                cg = (step - 2) * FF_CHUNK
                part = jnp.dot(acts.pop((s, step - 2)), wd_ref[0, cg:cg + FF_CHUNK, :],
                               preferred_element_type=F32)
                accs[s] = part if accs[s] is None else accs[s] + part
    for s in range(n_sub):
        out_ref[0, s * sub:(s + 1) * sub, :] = (xn_ref[xh + s * sub:xh + (s + 1) * sub, :]
                                                + g2 * accs[s])


def _tail(o, x, mod, g2n, w_o, wo_layer, w_up, conv_w, conv_b, w_down, layer, *, tile, alias):
    b, l, d = x.shape
    nmod = mod.shape[0]
    nt = l // tile
    ob = tile // BF16_SUBLANES
    xb = tile // SUBLANES
    mod_map = (lambda bi, ti: (bi, 0, 0)) if nmod > 1 else (lambda bi, ti: (0, 0, 0))
    row = lambda bi, ti: (bi, ti, 0)
    lay = lambda bi, ti: (layer, 0, 0)
    lay_o = lambda bi, ti: (wo_layer, 0, 0)
    prev = lambda nb: (lambda bi, ti: (bi, jnp.maximum(ti * nb - 1, 0), 0))
    nxt = lambda nb: (lambda bi, ti: (bi, jnp.minimum((ti + 1) * nb, nt * nb - 1), 0))
    return pl.pallas_call(
        functools.partial(_tail_kernel, tile=tile, sub=min(TAIL_SUB, tile), tiles_per_seq=nt),
        grid=(b, nt),
        in_specs=[pl.BlockSpec((1, BF16_SUBLANES, d), prev(ob)),
                  pl.BlockSpec((1, tile, d), row),
                  pl.BlockSpec((1, BF16_SUBLANES, d), nxt(ob)),
                  pl.BlockSpec((1, SUBLANES, d), prev(xb)),
                  pl.BlockSpec((1, tile, d), row),
                  pl.BlockSpec((1, SUBLANES, d), nxt(xb)),
                  pl.BlockSpec((1, MOD_ROWS, d), mod_map),
                  pl.BlockSpec((1, d), lambda bi, ti: (0, 0)),
                  _resident((1, d, d), lay_o),
                  _resident((1, d, 2 * D_FF), lay),
                  pl.BlockSpec((1, CONV_W, 2 * D_FF), lay),
                  pl.BlockSpec((1, 1, 2 * D_FF), lay),
                  _resident((1, D_FF, d), lay)],
        out_specs=pl.BlockSpec((1, tile, d), row),
        out_shape=jax.ShapeDtypeStruct((b, l, d), F32),
        scratch_shapes=[pltpu.VMEM((tile + 2 * BF16_SUBLANES, d), BF16),
                        pltpu.VMEM((tile + 2 * SUBLANES, d), F32),
                        pltpu.VMEM((tile + 2 * SUBLANES, d), BF16)],
        input_output_aliases={4: 0} if alias else {},
        compiler_params=_cparams(("arbitrary", "arbitrary")),
        name="layer_tail",
    )(o, o, o, x, x, x, mod, g2n, w_o, w_up, conv_w, conv_b, w_down)


def _rope_tables(n_lat):
    t = jnp.arange(n_lat)
    row = (t // GRID_W).astype(F32)
    col = (t % GRID_W).astype(F32)
    n_freq = HEAD_DIM // 4
    inv_freq = ROPE_THETA ** (-jnp.arange(n_freq, dtype=F32) / n_freq)
    ang = jnp.concatenate([row[:, None] * inv_freq, col[:, None] * inv_freq], axis=-1)
    cos = jnp.cos(ang)
    sin = jnp.sin(ang)
    return (jnp.concatenate([cos, cos, cos, cos], axis=-1),
            jnp.concatenate([-sin, -sin, sin, sin], axis=-1))


def _pair_layout(w):
    lead = w.shape[:-1]
    w = w.reshape(lead + (-1, 2, 2, QUARTER))
    return jnp.swapaxes(w, -3, -2).reshape(lead + (-1,))


def _dup_layout(w):
    lead = w.shape[:-1]
    w = w.reshape(lead + (-1, 2, 1, QUARTER))
    return jnp.broadcast_to(w, w.shape[:-2] + (2, QUARTER)).reshape(lead + (-1,))


def _head_mean_matrix():
    lane = np.arange(MXU_W)
    head = (lane // LANES) * 2 + (lane // QUARTER) % 2
    return jnp.asarray((head[:, None] == head[None, :]).astype(np.float32) / HEAD_DIM, dtype=BF16)


def _prep_qkv(w_qkv, qk_g, diff):
    n_q = D_MODEL
    w_q = _pair_layout(w_qkv[:, :n_q])
    if diff:
        n_kv = D_MODEL
        w_k = _pair_layout(w_qkv[:, n_q:n_q + n_kv])
        w_v = w_qkv[:, n_q + n_kv:]
        v_add = None
    else:
        n_kv = GQA_KV_HEADS * HEAD_DIM
        w_k = _dup_layout(w_qkv[:, n_q:n_q + n_kv])
        w_v = w_qkv[:, n_q + n_kv:].reshape(D_MODEL, GQA_KV_HEADS, HEAD_DIM)
        w_v = jnp.pad(w_v, ((0, 0), (0, 0), (0, LANES - HEAD_DIM))).reshape(D_MODEL, -1)
        one_col = np.zeros((LANES,), np.float32)
        one_col[HEAD_DIM] = 1.0
        v_add = jnp.asarray(np.tile(one_col, GQA_KV_HEADS)).reshape(1, -1)
    w = jnp.concatenate([w_q, w_k, w_v], axis=1)
    gain = jnp.concatenate([jnp.tile(_dup_layout(qk_g[0] * Q_SCALE), n_q // LANES),
                            jnp.tile(_dup_layout(qk_g[1]), w_k.shape[1] // LANES)])
    return w, gain.reshape(1, -1), v_add, n_q, n_q + w_k.shape[1]


def kernel(x, c, ctx, c_ctx, adaln_w, adaln_b, norm1_g, norm2_g, ffn_w_up, ffn_conv_w, ffn_conv_b, ffn_w_down, a_w_qkv, a_qk_g, a_lambda, a_head_g, a_w_o, b_w_qkv, b_qk_g, b_w_o, c_w_qkv, c_qk_g, c_sink, c_w_o):
    b, n_lat, d = x.shape
    n_ctx = ctx.shape[1]
    depth = adaln_w.shape[0]
    assert d == D_MODEL and b + 1 <= C_ROWS

    tabs = _rope_tables(n_lat)
    bd = _head_mean_matrix()

    cvec = jnp.concatenate([c, c_ctx[None, :], jnp.zeros((C_ROWS - b - 1, d), F32)], axis=0)
    mod_all = _adaln(cvec, adaln_w, adaln_b).reshape(depth, C_ROWS, 6, d)
    mod_all = jnp.pad(mod_all, ((0, 0), (0, 0), (0, MOD_ROWS - 6), (0, 0)))

    w_up_all = _to_bf16(ffn_w_up)
    w_down_all = _to_bf16(ffn_w_down)
    conv_b_all = ffn_conv_b.reshape(depth, 1, 2 * D_FF)
    qkv_all = [_to_bf16(w) for w in (a_w_qkv, b_w_qkv, c_w_qkv)]
    wo_all = [_to_bf16(w) for w in (a_w_o, b_w_o, c_w_o)]
    qk_g_all = (a_qk_g, b_qk_g, c_qk_g)

    qkv_tile = min(512, n_lat)
    tail_tile = min(512, n_lat)
    tq_lat = min(256, n_lat)
    tq_big = min(512, n_lat)
    tq_ctx = min(256, n_ctx)
    trip = lambda lq, tq, want: max(p for p in (1, 2, 4, 8) if p <= want and (lq // tq) % p == 0)

    h_ctx = ctx
    for i in range(depth):
        last = i == depth - 1
        j = i // N_MIXERS
        kind = i % N_MIXERS
        mod_l = mod_all[i, :b]
        mod_c = mod_all[i, b:b + 1]
        g1n = norm1_g[i].reshape(1, d)
        g2n = norm2_g[i].reshape(1, d)
        w_all, gain, v_add, n_q, n_norm = _prep_qkv(qkv_all[kind][j], qk_g_all[kind][j], kind == 0)

        q_l, k_l, v_l = _qkv_proj(x, mod_l, g1n, w_all, gain, bd, tabs, v_add,
                                  n_q=n_q, n_norm=n_norm, tile=qkv_tile)
        q_c, k_c, v_c = _qkv_proj(h_ctx, mod_c, g1n, w_all, gain, bd, None, v_add,
                                  n_q=n_q, n_norm=n_norm, tile=n_ctx)

        o_c = None
        if kind == 0:
            lam_init = 0.8 - 0.6 * math.exp(-0.3 * i)
            hg = a_head_g[j].reshape(1, DIFF_V_DIM)
            o_l = _diff_attention(a_lambda[j], hg, q_l, [(k_l, v_l), (k_c, v_c)], lam_init=lam_init,
                                  tq=tq_lat, per_trip=trip(n_lat, tq_lat, 8), heads_per_step=1)
            if not last:
                o_c = _diff_attention(a_lambda[j], hg, q_c, [(k_c, v_c)], lam_init=lam_init,
                                      tq=tq_ctx // 2, per_trip=2, heads_per_step=DIFF_HEADS)
        elif kind == 1:
            o_l = _gqa_dense_attention(q_l, [(k_l, v_l), (k_c, v_c)], tq=tq_big,
                                       per_trip=trip(n_lat, tq_big, 2))
            if not last:
                o_c = _gqa_dense_attention(q_c, [(k_c, v_c)], tq=tq_ctx, per_trip=1)
        else:
            sink = c_sink[j].reshape(GQA_KV_HEADS, GQA_GROUP)
            o_l = _gqa_window_attention(q_l, k_l, v_l, k_c, v_c, sink, tq=tq_lat,
                                        per_trip=trip(n_lat, tq_lat, 2))
            if not last:
                o_c = _gqa_dense_attention(q_c, [(k_c, v_c)], tq=tq_ctx, per_trip=1, sink=sink)

        tail = functools.partial(_tail, g2n=g2n, w_o=wo_all[kind], wo_layer=j, w_up=w_up_all,
                                 conv_w=ffn_conv_w, conv_b=conv_b_all, w_down=w_down_all, layer=i,
                                 alias=False)
        x = tail(o_l, x, mod_l, tile=tail_tile)
        if not last:
            h_ctx = tail(o_c, h_ctx, mod_c, tile=n_ctx)
    return x
```

```python
import functools
import math

import numpy as np
import jax
import jax.numpy as jnp
from jax import lax
from jax.experimental import pallas as pl
from jax.experimental.pallas import tpu as pltpu

F32 = jnp.float32
BF16 = jnp.bfloat16

D_MODEL = 1024
HEAD_DIM = 64
GRID_W = 64
DIFF_HEADS = D_MODEL // (2 * HEAD_DIM)
DIFF_V_DIM = 2 * HEAD_DIM
GQA_HEADS = D_MODEL // HEAD_DIM
GQA_KV_HEADS = GQA_HEADS // 4
GQA_GROUP = GQA_HEADS // GQA_KV_HEADS
WINDOW = 128
D_FF = 256 * ((8 * D_MODEL // 3 + 255) // 256)
CONV_W = 3
ROPE_THETA = 10000.0
EPS = 1e-6
ATTN_SCALE = HEAD_DIM ** -0.5
LOG2E = 1.0 / math.log(2.0)
Q_SCALE = ATTN_SCALE * LOG2E
N_MIXERS = 3

LANES = 128
SUBLANES = 8
QUARTER = HEAD_DIM // 2
BF16_SUBLANES = 16
MXU_W = 256
FF_CHUNK = MXU_W
TAIL_SUB = 256
MOD_ROWS = 8
C_ROWS = 24
VMEM_LIMIT = 56 * 1024 * 1024
CAST_BLOCK_BYTES = 4 * 1024 * 1024
NEG_BIG = -1e30


def _cparams(sem, vmem=VMEM_LIMIT):
    return pltpu.CompilerParams(dimension_semantics=sem, vmem_limit_bytes=vmem)


def _resident(shape, index_map):
    return pl.BlockSpec(shape, index_map, pipeline_mode=pl.Buffered(1))


def _rms(x):
    return x * lax.rsqrt(jnp.mean(x * x, axis=-1, keepdims=True) + EPS)


def _head_a_mask():
    lane = lax.broadcasted_iota(jnp.int32, (1, LANES), 1)
    return (lane & QUARTER) == 0


def _cast_kernel(x_ref, o_ref):
    o_ref[...] = x_ref[...].astype(BF16)


def _to_bf16(w):
    n, r, c = w.shape
    parts = 1
    while (r // parts) * c * 4 > CAST_BLOCK_BYTES or r % parts or (r // parts) % BF16_SUBLANES:
        parts += 1
    tr = r // parts
    blk = lambda i, j: (i, j, 0)
    return pl.pallas_call(
        _cast_kernel,
        grid=(n, parts),
        in_specs=[pl.BlockSpec((1, tr, c), blk)],
        out_specs=pl.BlockSpec((1, tr, c), blk),
        out_shape=jax.ShapeDtypeStruct(w.shape, BF16),
        compiler_params=_cparams(("arbitrary", "arbitrary"), 32 * 1024 * 1024),
        name="cast_bf16",
    )(w)


def _adaln_kernel(c_ref, w_ref, b_ref, o_ref):
    c = c_ref[...]
    s = c * jax.nn.sigmoid(c)
    o_ref[0] = jnp.dot(s.astype(BF16), w_ref[0].astype(BF16),
                       preferred_element_type=F32) + b_ref[0]


def _adaln(cvec, adaln_w, adaln_b):
    depth, d, n = adaln_w.shape
    tn = 1536
    return pl.pallas_call(
        _adaln_kernel,
        grid=(depth, n // tn),
        in_specs=[pl.BlockSpec((C_ROWS, d), lambda i, j: (0, 0)),
                  pl.BlockSpec((1, d, tn), lambda i, j: (i, 0, j)),
                  pl.BlockSpec((1, 1, tn), lambda i, j: (i, 0, j))],
        out_specs=pl.BlockSpec((1, C_ROWS, tn), lambda i, j: (i, 0, j)),
        out_shape=jax.ShapeDtypeStruct((depth, C_ROWS, n), F32),
        compiler_params=_cparams(("arbitrary", "arbitrary"), 40 * 1024 * 1024),
        name="adaln",
    )(cvec, adaln_w, adaln_b.reshape(depth, 1, n))


def _qkv_kernel(*refs, rope, n_q, n_norm, v_add):
    x_ref, mod_ref, g_ref, w_ref, gain_ref, bd_ref = refs[:6]
    pos = 6
    if rope:
        cos = refs[pos][...]
        sin = refs[pos + 1][...]
        pos += 2
    if v_add:
        va_ref = refs[pos]
        pos += 1
    q_ref, k_ref, v_ref = refs[pos:]
    n_total = w_ref.shape[1]

    x = x_ref[0]
    sh = mod_ref[0, 0:1, :]
    sc = mod_ref[0, 1:2, :]
    hn = ((_rms(x) * g_ref[...]) * (1.0 + sc) + sh).astype(BF16)

    def project(c0):
        return jnp.dot(hn, w_ref[:, c0:c0 + MXU_W], preferred_element_type=F32)

    def mean_sq(z):
        return jnp.dot((z * z).astype(BF16), bd_ref[...], preferred_element_type=F32)

    def finish(c0, z, msq):
        if c0 < n_norm:
            z = z * lax.rsqrt(msq + EPS) * gain_ref[:, c0:c0 + MXU_W]
            if rope:
                z = jnp.concatenate(
                    [z[:, s0:s0 + LANES] * cos + pltpu.roll(z[:, s0:s0 + LANES], HEAD_DIM, 1) * sin
                     for s0 in range(0, MXU_W, LANES)], axis=-1)
            if c0 < n_q:
                q_ref[0, :, c0:c0 + MXU_W] = z.astype(BF16)
            else:
                k_ref[0, :, c0 - n_q:c0 - n_q + MXU_W] = z.astype(BF16)
        else:
            if v_add:
                z = z + va_ref[:, c0 - n_norm:c0 - n_norm + MXU_W]
            v_ref[0, :, c0 - n_norm:c0 - n_norm + MXU_W] = z.astype(BF16)

    starts = list(range(0, n_total, MXU_W))
    zs, ms = {}, {}
    for step in range(len(starts) + 2):
        if step < len(starts):
            zs[step] = project(starts[step])
        if 0 <= step - 1 < len(starts) and starts[step - 1] < n_norm:
            ms[step - 1] = mean_sq(zs[step - 1])
        if 0 <= step - 2 < len(starts):
            finish(starts[step - 2], zs.pop(step - 2), ms.pop(step - 2, None))


def _qkv_proj(x, mod, g, w, gain, bd, tabs, v_add, *, n_q, n_norm, tile):
    b, l, d = x.shape
    nmod = mod.shape[0]
    n = w.shape[1]
    n_k = n_norm - n_q
    n_v = n - n_norm
    rope = tabs is not None
    mod_map = (lambda bi, ti: (bi, 0, 0)) if nmod > 1 else (lambda bi, ti: (0, 0, 0))
    const2 = lambda bi, ti: (0, 0)
    row = lambda bi, ti: (bi, ti, 0)
    in_specs = [pl.BlockSpec((1, tile, d), row),
                pl.BlockSpec((1, MOD_ROWS, d), mod_map),
                pl.BlockSpec((1, d), const2),
                _resident((d, n), const2),
                pl.BlockSpec((1, n_norm), const2),
                pl.BlockSpec((MXU_W, MXU_W), const2)]
    args = [x, mod, g, w, gain, bd]
    if rope:
        in_specs += [pl.BlockSpec((tile, LANES), lambda bi, ti: (ti, 0))] * 2
        args += list(tabs)
    if v_add is not None:
        in_specs.append(pl.BlockSpec((1, n_v), const2))
        args.append(v_add)
    return pl.pallas_call(
        functools.partial(_qkv_kernel, rope=rope, n_q=n_q, n_norm=n_norm, v_add=v_add is not None),
        grid=(b, l // tile),
        in_specs=in_specs,
        out_specs=[pl.BlockSpec((1, tile, n_q), row), pl.BlockSpec((1, tile, n_k), row),
                   pl.BlockSpec((1, tile, n_v), row)],
        out_shape=[jax.ShapeDtypeStruct((b, l, n_q), BF16), jax.ShapeDtypeStruct((b, l, n_k), BF16),
                   jax.ShapeDtypeStruct((b, l, n_v), BF16)],
        compiler_params=_cparams(("arbitrary", "arbitrary")),
        name="qkv_proj",
    )(*args)


def _dot_nt(a, b):
    return lax.dot_general(a, b, (((1,), (1,)), ((), ())), preferred_element_type=F32)


def _row_max(s_list):
    m = s_list[0].max(axis=-1, keepdims=True)
    for s in s_list[1:]:
        m = jnp.maximum(m, s.max(axis=-1, keepdims=True))
    return m


def _split_heads(qp):
    mask_a = _head_a_mask()
    zero = jnp.zeros_like(qp)
    return jnp.where(mask_a, qp, zero), jnp.where(mask_a, zero, qp)


def _softmax_pv(s_list, v_list, sink_logit, dv):
    m = _row_max(s_list)
    if sink_logit is not None:
        m = jnp.maximum(m, sink_logit)
    o = None
    for s, v in zip(s_list, v_list):
        oj = jnp.dot(jnp.exp2(s - m).astype(BF16), v, preferred_element_type=F32)
        o = oj if o is None else o + oj
    den = o[:, dv:dv + 1]
    if sink_logit is not None:
        den = den + jnp.exp2(sink_logit - m)
    return o[:, :dv] * (1.0 / den)


def _head_tiles(q_ref, tq, per_trip, tile_ctx, logits, finish, emit, lookahead=1):
    nq = q_ref.shape[1] // tq
    assert nq % per_trip == 0
    n_heads = 2 * (q_ref.shape[2] // LANES)

    def heads(r0):
        hs = []
        for p0 in range(0, q_ref.shape[2], LANES):
            hs += list(_split_heads(q_ref[0, pl.ds(r0, tq), p0:p0 + LANES]))
        return hs

    def body(i, carry):
        pending = []
        outs = []

        def retire():
            r_done, ctx, s, g = pending.pop(0)
            outs.append(finish(ctx, s, g))
            if len(outs) == n_heads:
                emit(r_done, list(outs))
                outs.clear()

        for u in range(per_trip):
            r0 = pl.multiple_of((i * per_trip + u) * tq, tq)
            ctx = tile_ctx(r0)
            for g, q in enumerate(heads(r0)):
                pending.append((r0, ctx, logits(ctx, q, g), g))
                if len(pending) > lookahead:
                    retire()
        while pending:
            retire()
        return carry

    lax.fori_loop(0, nq // per_trip, body, 0)


def _diff_attn_kernel(*refs, lam_init, n_src, tq, per_trip):
    lam_ref, hg_ref, q_ref = refs[:3]
    kv_refs = refs[3:3 + 2 * n_src]
    o_ref = refs[3 + 2 * n_src]
    va_refs = refs[4 + 2 * n_src:]
    n_heads = q_ref.shape[2] // LANES
    lq = lam_ref[...]
    lam = (jnp.exp(jnp.sum(lq[0:1] * lq[1:2], axis=-1, keepdims=True))
           - jnp.exp(jnp.sum(lq[2:3] * lq[3:4], axis=-1, keepdims=True)) + lam_init)
    hg = hg_ref[...] * (1.0 - lam_init)
    lane = lax.broadcasted_iota(jnp.int32, (1, LANES), 1)
    for h in range(n_heads):
        for j in range(n_src):
            s = kv_refs[2 * j].shape[1]
            va = va_refs[h * n_src + j]
            va[:, :DIFF_V_DIM] = kv_refs[2 * j + 1][0, :, h * LANES:(h + 1) * LANES]
            va[:, DIFF_V_DIM:] = jnp.broadcast_to(
                jnp.where(lane == 0, 1.0, 0.0).astype(BF16), (s, LANES))

    def logits(ctx, q, g):
        h = g // 2
        return [_dot_nt(q, kv_refs[2 * j][0, :, h * LANES:(h + 1) * LANES]) for j in range(n_src)]

    def finish(ctx, s, g):
        h = g // 2
        return _softmax_pv(s, [va_refs[h * n_src + j][...] for j in range(n_src)], None,
                           DIFF_V_DIM)

    def emit(r0, outs):
        for h in range(n_heads):
            o = outs[2 * h] - lam * outs[2 * h + 1]
            o_ref[0, pl.ds(r0, tq), h * LANES:(h + 1) * LANES] = (_rms(o) * hg).astype(BF16)

    _head_tiles(q_ref, tq, per_trip, lambda r0: None, logits, finish, emit)


def _diff_attention(lam_p, head_g, q, kvs, *, lam_init, tq, per_trip, heads_per_step):
    b, lq, d = q.shape
    n_src = len(kvs)
    width = heads_per_step * LANES
    blk = lambda bi, h: (bi, 0, h)
    in_specs = [pl.BlockSpec((4, HEAD_DIM), lambda bi, h: (0, 0)),
                pl.BlockSpec((1, DIFF_V_DIM), lambda bi, h: (0, 0)),
                pl.BlockSpec((1, lq, width), blk)]
    args = [lam_p, head_g, q]
    for k, v in kvs:
        in_specs += [pl.BlockSpec((1, k.shape[1], width), blk)] * 2
        args += [k, v]
    return pl.pallas_call(
        functools.partial(_diff_attn_kernel, lam_init=lam_init, n_src=n_src, tq=tq,
                          per_trip=per_trip),
        grid=(b, DIFF_HEADS // heads_per_step),
        in_specs=in_specs,
        out_specs=pl.BlockSpec((1, lq, width), blk),
        out_shape=jax.ShapeDtypeStruct((b, lq, d), BF16),
        scratch_shapes=[pltpu.VMEM((k.shape[1], DIFF_V_DIM + LANES), BF16)
                        for _ in range(heads_per_step) for k, _ in kvs],
        compiler_params=_cparams(("arbitrary", "arbitrary")),
        name="diff_attn",
    )(*args)


def _gqa_emit(o_ref, tq):
    def emit(r0, outs):
        o_ref[0, pl.ds(r0, tq), :] = jnp.concatenate(outs, axis=-1).astype(BF16)
    return emit


def _gqa_dense_kernel(*refs, n_src, tq, per_trip, sink):
    if sink:
        sink_ref = refs[0]
        refs = refs[1:]
    q_ref = refs[0]
    kv_refs = refs[1:1 + 2 * n_src]
    o_ref = refs[1 + 2 * n_src]
    kk = pl.program_id(1)

    def logits(ctx, q, g):
        return [_dot_nt(q, kv_refs[2 * j][0]) for j in range(n_src)]

    def finish(ctx, s, g):
        v = [kv_refs[2 * j + 1][0] for j in range(n_src)]
        return _softmax_pv(s, v, sink_ref[kk, g] * LOG2E if sink else None, HEAD_DIM)

    _head_tiles(q_ref, tq, per_trip, lambda r0: None, logits, finish, _gqa_emit(o_ref, tq))


def _gqa_dense_attention(q, kvs, *, tq, per_trip, sink=None):
    b, lq, d = q.shape
    n_src = len(kvs)
    in_specs = []
    args = []
    if sink is not None:
        in_specs.append(pl.BlockSpec(memory_space=pltpu.SMEM))
        args.append(sink)
    blk = lambda bi, kk: (bi, 0, kk)
    q_width = GQA_GROUP * HEAD_DIM
    in_specs.append(pl.BlockSpec((1, lq, q_width), blk))
    args.append(q)
    for k, v in kvs:
        in_specs += [pl.BlockSpec((1, k.shape[1], LANES), blk)] * 2
        args += [k, v]
    return pl.pallas_call(
        functools.partial(_gqa_dense_kernel, n_src=n_src, tq=tq, per_trip=per_trip,
                          sink=sink is not None),
        grid=(b, GQA_KV_HEADS),
        in_specs=in_specs,
        out_specs=pl.BlockSpec((1, lq, q_width), blk),
        out_shape=jax.ShapeDtypeStruct((b, lq, d), BF16),
        compiler_params=_cparams(("arbitrary", "arbitrary")),
        name="gqa_dense_attn",
    )(*args)


def _gqa_window_kernel(sink_ref, q_ref, kl_ref, vl_ref, kc_ref, vc_ref, o_ref, *, tq, per_trip):
    n_lat = q_ref.shape[1]
    band = min(tq + 2 * WINDOW, n_lat)
    kk = pl.program_id(1)
    rel0 = (lax.broadcasted_iota(jnp.int32, (tq, band), 1)
            - lax.broadcasted_iota(jnp.int32, (tq, band), 0))

    def tile_ctx(r0):
        start = pl.multiple_of(jnp.clip(r0 - WINDOW, 0, n_lat - band), WINDOW)
        kb = kl_ref[0, pl.ds(start, band), :]
        vb = vl_ref[0, pl.ds(start, band), :]
        in_band = jnp.abs(rel0 + (start - r0)) <= WINDOW
        return kb, vb, in_band

    def logits(ctx, q, g):
        kb, _, in_band = ctx
        return [jnp.where(in_band, _dot_nt(q, kb), NEG_BIG), _dot_nt(q, kc_ref[0])]

    def finish(ctx, s, g):
        return _softmax_pv(s, [ctx[1], vc_ref[0]], sink_ref[kk, g] * LOG2E, HEAD_DIM)

    _head_tiles(q_ref, tq, per_trip, tile_ctx, logits, finish, _gqa_emit(o_ref, tq), lookahead=2)


def _gqa_window_attention(q, k_l, v_l, k_c, v_c, sink, *, tq, per_trip):
    b, lq, d = q.shape
    lc = k_c.shape[1]
    blk = lambda bi, kk: (bi, 0, kk)
    return pl.pallas_call(
        functools.partial(_gqa_window_kernel, tq=tq, per_trip=per_trip),
        grid=(b, GQA_KV_HEADS),
        in_specs=[pl.BlockSpec(memory_space=pltpu.SMEM),
                  pl.BlockSpec((1, lq, GQA_GROUP * HEAD_DIM), blk),
                  pl.BlockSpec((1, lq, LANES), blk),
                  pl.BlockSpec((1, lq, LANES), blk),
                  pl.BlockSpec((1, lc, LANES), blk),
                  pl.BlockSpec((1, lc, LANES), blk)],
        out_specs=pl.BlockSpec((1, lq, GQA_GROUP * HEAD_DIM), blk),
        out_shape=jax.ShapeDtypeStruct((b, lq, d), BF16),
        compiler_params=_cparams(("arbitrary", "arbitrary")),
        name="gqa_window_attn",
    )(sink, q, k_l, v_l, k_c, v_c)


def _tail_kernel(op_ref, o_ref, ox_ref, xp_ref, x_ref, xx_ref, mod_ref, g_ref, wo_ref, wu_ref,
                 cw_ref, cb_ref, wd_ref, out_ref, olhs_ref, xn_ref, lhs_ref, act_ref, *, tile, sub,
                 tiles_per_seq):
    ti = pl.program_id(1)
    oh = BF16_SUBLANES
    xh = SUBLANES
    m = tile + 2 * xh
    olhs_ref[0:oh, :] = op_ref[0]
    olhs_ref[oh:oh + tile, :] = o_ref[0]
    olhs_ref[oh + tile:, :] = ox_ref[0]
    y = jnp.dot(olhs_ref[...], wo_ref[0], preferred_element_type=F32)[oh - xh:oh + tile + xh]
    g1 = mod_ref[0, 2:3, :]
    sh2 = mod_ref[0, 3:4, :]
    sc2 = mod_ref[0, 4:5, :]
    g2 = mod_ref[0, 5:6, :]
    xn = jnp.concatenate([xp_ref[0], x_ref[0], xx_ref[0]], axis=0) + g1 * y
    xn_ref[...] = xn
    hn = (_rms(xn) * g_ref[...]) * (1.0 + sc2) + sh2
    row = lax.broadcasted_iota(jnp.int32, (m, 1), 0)
    keep = jnp.logical_and(jnp.logical_or(ti > 0, row >= xh),
                           jnp.logical_or(ti < tiles_per_seq - 1, row < xh + tile))
    lhs_ref[...] = jnp.where(keep, hn, 0.0).astype(BF16)

    ms = sub + 2 * xh
    n_sub = tile // sub

    def conv(u, c0):
        prev = pltpu.roll(u, 1, 0)[xh:xh + sub]
        nxt = pltpu.roll(u, ms - 1, 0)[xh:xh + sub]
        cur = u[xh:xh + sub]
        return (prev * cw_ref[0, 0:1, c0:c0 + FF_CHUNK] + cur * cw_ref[0, 1:2, c0:c0 + FF_CHUNK]
                + nxt * cw_ref[0, 2:3, c0:c0 + FF_CHUNK] + cb_ref[0, :, c0:c0 + FF_CHUNK])

    def up(s, j):
        lhs = lhs_ref[s * sub:s * sub + ms, :]
        cg = j * FF_CHUNK
        cv = D_FF + j * FF_CHUNK
        return (jnp.dot(lhs, wu_ref[0, :, cg:cg + FF_CHUNK], preferred_element_type=F32),
                jnp.dot(lhs, wu_ref[0, :, cv:cv + FF_CHUNK], preferred_element_type=F32))

    def gated(j, u):
        cg = j * FF_CHUNK
        gate = conv(u[0], cg)
        val = conv(u[1], D_FF + cg)
        return (gate * jax.nn.sigmoid(gate) * val).astype(BF16)

    n_chunks = D_FF // FF_CHUNK
    split = (n_chunks + 1) // 2
    accs = [None] * n_sub
    us = {}

    def down(s, c_lo, c_hi):
        lo, hi = c_lo * FF_CHUNK, c_hi * FF_CHUNK
        part = jnp.dot(act_ref[s, :, lo:hi], wd_ref[0, lo:hi, :], preferred_element_type=F32)
        accs[s] = part if accs[s] is None else accs[s] + part

    for step in range(n_chunks + 1):
        for s in range(n_sub):
            if step < n_chunks:
                us[s, step] = up(s, step)
            if 0 <= step - 1 < n_chunks:
                j = step - 1
                act_ref[s, :, j * FF_CHUNK:(j + 1) * FF_CHUNK] = gated(j, us.pop((s, j)))
                if j == split - 1:
                    down(s, 0, split)
    for s in range(n_sub):
        down(s, split, n_chunks)
        out_ref[0, s * sub:(s + 1) * sub, :] = (xn_ref[xh + s * sub:xh + (s + 1) * sub, :]
                                                + g2 * accs[s])


def _tail(o, x, mod, g2n, w_o, wo_layer, w_up, conv_w, conv_b, w_down, layer, *, tile):
    b, l, d = x.shape
    nmod = mod.shape[0]
    nt = l // tile
    ob = tile // BF16_SUBLANES
    xb = tile // SUBLANES
    mod_map = (lambda bi, ti: (bi, 0, 0)) if nmod > 1 else (lambda bi, ti: (0, 0, 0))
    row = lambda bi, ti: (bi, ti, 0)
    lay = lambda bi, ti: (layer, 0, 0)
    lay_o = lambda bi, ti: (wo_layer, 0, 0)
    prev = lambda nb: (lambda bi, ti: (bi, jnp.maximum(ti * nb - 1, 0), 0))
    nxt = lambda nb: (lambda bi, ti: (bi, jnp.minimum((ti + 1) * nb, nt * nb - 1), 0))
    sub = min(TAIL_SUB, tile)
    return pl.pallas_call(
        functools.partial(_tail_kernel, tile=tile, sub=sub, tiles_per_seq=nt),
        grid=(b, nt),
        in_specs=[pl.BlockSpec((1, BF16_SUBLANES, d), prev(ob)),
                  pl.BlockSpec((1, tile, d), row),
                  pl.BlockSpec((1, BF16_SUBLANES, d), nxt(ob)),
                  pl.BlockSpec((1, SUBLANES, d), prev(xb)),
                  pl.BlockSpec((1, tile, d), row),
                  pl.BlockSpec((1, SUBLANES, d), nxt(xb)),
                  pl.BlockSpec((1, MOD_ROWS, d), mod_map),
                  pl.BlockSpec((1, d), lambda bi, ti: (0, 0)),
                  _resident((1, d, d), lay_o),
                  _resident((1, d, 2 * D_FF), lay),
                  pl.BlockSpec((1, CONV_W, 2 * D_FF), lay),
                  pl.BlockSpec((1, 1, 2 * D_FF), lay),
                  _resident((1, D_FF, d), lay)],
        out_specs=pl.BlockSpec((1, tile, d), row),
        out_shape=jax.ShapeDtypeStruct((b, l, d), F32),
        scratch_shapes=[pltpu.VMEM((tile + 2 * BF16_SUBLANES, d), BF16),
                        pltpu.VMEM((tile + 2 * SUBLANES, d), F32),
                        pltpu.VMEM((tile + 2 * SUBLANES, d), BF16),
                        pltpu.VMEM((tile // sub, sub, D_FF), BF16)],
        compiler_params=_cparams(("arbitrary", "arbitrary")),
        name="layer_tail",
    )(o, o, o, x, x, x, mod, g2n, w_o, w_up, conv_w, conv_b, w_down)


def _rope_tables(n_lat):
    t = jnp.arange(n_lat)
    row = (t // GRID_W).astype(F32)
    col = (t % GRID_W).astype(F32)
    n_freq = HEAD_DIM // 4
    inv_freq = ROPE_THETA ** (-jnp.arange(n_freq, dtype=F32) / n_freq)
    ang = jnp.concatenate([row[:, None] * inv_freq, col[:, None] * inv_freq], axis=-1)
    cos = jnp.cos(ang)
    sin = jnp.sin(ang)
    return (jnp.concatenate([cos, cos, cos, cos], axis=-1),
            jnp.concatenate([-sin, -sin, sin, sin], axis=-1))


def _pair_layout(w):
    lead = w.shape[:-1]
    w = w.reshape(lead + (-1, 2, 2, QUARTER))
    return jnp.swapaxes(w, -3, -2).reshape(lead + (-1,))


def _dup_layout(w):
    lead = w.shape[:-1]
    w = w.reshape(lead + (-1, 2, 1, QUARTER))
    return jnp.broadcast_to(w, w.shape[:-2] + (2, QUARTER)).reshape(lead + (-1,))


def _head_mean_matrix():
    lane = np.arange(MXU_W)
    head = (lane // LANES) * 2 + (lane // QUARTER) % 2
    return jnp.asarray((head[:, None] == head[None, :]).astype(np.float32) / HEAD_DIM, dtype=BF16)


def _prep_qkv(w_qkv, qk_g, diff):
    n_q = D_MODEL
    w_q = _pair_layout(w_qkv[:, :n_q])
    if diff:
        n_kv = D_MODEL
        w_k = _pair_layout(w_qkv[:, n_q:n_q + n_kv])
        w_v = w_qkv[:, n_q + n_kv:]
        v_add = None
    else:
        n_kv = GQA_KV_HEADS * HEAD_DIM
        w_k = _dup_layout(w_qkv[:, n_q:n_q + n_kv])
        w_v = w_qkv[:, n_q + n_kv:].reshape(D_MODEL, GQA_KV_HEADS, HEAD_DIM)
        w_v = jnp.pad(w_v, ((0, 0), (0, 0), (0, LANES - HEAD_DIM))).reshape(D_MODEL, -1)
        one_col = np.zeros((LANES,), np.float32)
        one_col[HEAD_DIM] = 1.0
        v_add = jnp.asarray(np.tile(one_col, GQA_KV_HEADS)).reshape(1, -1)
    w = jnp.concatenate([w_q, w_k, w_v], axis=1)
    gain = jnp.concatenate([jnp.tile(_dup_layout(qk_g[0] * Q_SCALE), n_q // LANES),
                            jnp.tile(_dup_layout(qk_g[1]), w_k.shape[1] // LANES)])
    return w, gain.reshape(1, -1), v_add, n_q, n_q + w_k.shape[1]


def kernel(x, c, ctx, c_ctx, adaln_w, adaln_b, norm1_g, norm2_g, ffn_w_up, ffn_conv_w, ffn_conv_b, ffn_w_down, a_w_qkv, a_qk_g, a_lambda, a_head_g, a_w_o, b_w_qkv, b_qk_g, b_w_o, c_w_qkv, c_qk_g, c_sink, c_w_o):
    b, n_lat, d = x.shape
    n_ctx = ctx.shape[1]
    depth = adaln_w.shape[0]
    assert d == D_MODEL and b + 1 <= C_ROWS

    tabs = _rope_tables(n_lat)
    bd = _head_mean_matrix()

    cvec = jnp.concatenate([c, c_ctx[None, :], jnp.zeros((C_ROWS - b - 1, d), F32)], axis=0)
    mod_all = _adaln(cvec, adaln_w, adaln_b).reshape(depth, C_ROWS, 6, d)
    mod_all = jnp.pad(mod_all, ((0, 0), (0, 0), (0, MOD_ROWS - 6), (0, 0)))

    w_up_all = _to_bf16(ffn_w_up)
    w_down_all = _to_bf16(ffn_w_down)
    conv_b_all = ffn_conv_b.reshape(depth, 1, 2 * D_FF)
    qkv_all = [_to_bf16(w) for w in (a_w_qkv, b_w_qkv, c_w_qkv)]
    wo_all = [_to_bf16(w) for w in (a_w_o, b_w_o, c_w_o)]
    qk_g_all = (a_qk_g, b_qk_g, c_qk_g)

    qkv_tile = min(512, n_lat)
    tail_tile = min(512, n_lat)
    tq_lat = min(256, n_lat)
    tq_big = min(512, n_lat)
    tq_ctx = min(256, n_ctx)
    trip = lambda lq, tq, want: max(p for p in (1, 2, 4, 8) if p <= want and (lq // tq) % p == 0)

    h_ctx = ctx
    for i in range(depth):
        last = i == depth - 1
        j = i // N_MIXERS
        kind = i % N_MIXERS
        mod_l = mod_all[i, :b]
        mod_c = mod_all[i, b:b + 1]
        g1n = norm1_g[i].reshape(1, d)
        g2n = norm2_g[i].reshape(1, d)
        w_all, gain, v_add, n_q, n_norm = _prep_qkv(qkv_all[kind][j], qk_g_all[kind][j], kind == 0)

        q_l, k_l, v_l = _qkv_proj(x, mod_l, g1n, w_all, gain, bd, tabs, v_add,
                                  n_q=n_q, n_norm=n_norm, tile=qkv_tile)
        q_c, k_c, v_c = _qkv_proj(h_ctx, mod_c, g1n, w_all, gain, bd, None, v_add,
                                  n_q=n_q, n_norm=n_norm, tile=n_ctx)

        o_c = None
        if kind == 0:
            lam_init = 0.8 - 0.6 * math.exp(-0.3 * i)
            hg = a_head_g[j].reshape(1, DIFF_V_DIM)
            o_l = _diff_attention(a_lambda[j], hg, q_l, [(k_l, v_l), (k_c, v_c)], lam_init=lam_init,
                                  tq=tq_lat, per_trip=trip(n_lat, tq_lat, 8), heads_per_step=1)
            if not last:
                o_c = _diff_attention(a_lambda[j], hg, q_c, [(k_c, v_c)], lam_init=lam_init,
                                      tq=tq_ctx // 2, per_trip=2, heads_per_step=DIFF_HEADS)
        elif kind == 1:
            o_l = _gqa_dense_attention(q_l, [(k_l, v_l), (k_c, v_c)], tq=tq_big,
                                       per_trip=trip(n_lat, tq_big, 2))
            if not last:
                o_c = _gqa_dense_attention(q_c, [(k_c, v_c)], tq=tq_ctx, per_trip=1)
        else:
            sink = c_sink[j].reshape(GQA_KV_HEADS, GQA_GROUP)
            o_l = _gqa_window_attention(q_l, k_l, v_l, k_c, v_c, sink, tq=tq_lat,
                                        per_trip=trip(n_lat, tq_lat, 2))
            if not last:
                o_c = _gqa_dense_attention(q_c, [(k_c, v_c)], tq=tq_ctx, per_trip=1, sink=sink)

        tail = functools.partial(_tail, g2n=g2n, w_o=wo_all[kind], wo_layer=j, w_up=w_up_all,
                                 conv_w=ffn_conv_w, conv_b=conv_b_all, w_down=w_down_all, layer=i)
        x = tail(o_l, x, mod_l, tile=tail_tile)
        if not last:
            h_ctx = tail(o_c, h_ctx, mod_c, tile=n_ctx)
    return x
```

```python
import functools
import math

import numpy as np
import jax
import jax.numpy as jnp
from jax import lax
from jax.experimental import pallas as pl
from jax.experimental.pallas import tpu as pltpu

F32 = jnp.float32
BF16 = jnp.bfloat16

D_MODEL = 1024
HEAD_DIM = 64
GRID_W = 64
DIFF_HEADS = D_MODEL // (2 * HEAD_DIM)
DIFF_V_DIM = 2 * HEAD_DIM
GQA_HEADS = D_MODEL // HEAD_DIM
GQA_KV_HEADS = GQA_HEADS // 4
GQA_GROUP = GQA_HEADS // GQA_KV_HEADS
WINDOW = 128
D_FF = 256 * ((8 * D_MODEL // 3 + 255) // 256)
CONV_W = 3
ROPE_THETA = 10000.0
EPS = 1e-6
ATTN_SCALE = HEAD_DIM ** -0.5
LOG2E = 1.0 / math.log(2.0)
Q_SCALE = ATTN_SCALE * LOG2E
N_MIXERS = 3

LANES = 128
SUBLANES = 8
QUARTER = HEAD_DIM // 2
BF16_SUBLANES = 16
MXU_W = 256
FF_CHUNK = MXU_W
TAIL_SUB = 256
DOWN_PARTS = 2
MOD_ROWS = 8
C_ROWS = 24
VMEM_LIMIT = 56 * 1024 * 1024
STREAM_VMEM_LIMIT = 40 * 1024 * 1024
CAST_BLOCK_BYTES = 4 * 1024 * 1024
ADALN_COLS = 1536
NEG_BIG = -1e30


def _cparams(sem, vmem=VMEM_LIMIT):
    return pltpu.CompilerParams(dimension_semantics=sem, vmem_limit_bytes=vmem)


def _resident(shape, index_map):
    return pl.BlockSpec(shape, index_map, pipeline_mode=pl.Buffered(1))


def _rms(x):
    return x * lax.rsqrt(jnp.mean(x * x, axis=-1, keepdims=True) + EPS)


def _head_a_mask():
    lane = lax.broadcasted_iota(jnp.int32, (1, LANES), 1)
    return (lane & QUARTER) == 0


def _cast_kernel(x_ref, o_ref):
    o_ref[...] = x_ref[...].astype(BF16)


def _to_bf16(w):
    n, r, c = w.shape
    parts = 1
    while (r // parts) * c * 4 > CAST_BLOCK_BYTES or r % parts or (r // parts) % BF16_SUBLANES:
        parts += 1
    tr = r // parts
    blk = lambda i, j: (i, j, 0)
    return pl.pallas_call(
        _cast_kernel,
        grid=(n, parts),
        in_specs=[pl.BlockSpec((1, tr, c), blk)],
        out_specs=pl.BlockSpec((1, tr, c), blk),
        out_shape=jax.ShapeDtypeStruct(w.shape, BF16),
        compiler_params=_cparams(("arbitrary", "arbitrary"), STREAM_VMEM_LIMIT),
        name="cast_bf16",
    )(w)


def _adaln_kernel(c_ref, w_ref, b_ref, o_ref):
    c = c_ref[...]
    s = c * jax.nn.sigmoid(c)
    o_ref[0] = jnp.dot(s.astype(BF16), w_ref[0].astype(BF16),
                       preferred_element_type=F32) + b_ref[0]


def _adaln(cvec, adaln_w, adaln_b):
    depth, d, n = adaln_w.shape
    tn = ADALN_COLS
    return pl.pallas_call(
        _adaln_kernel,
        grid=(depth, n // tn),
        in_specs=[pl.BlockSpec((C_ROWS, d), lambda i, j: (0, 0)),
                  pl.BlockSpec((1, d, tn), lambda i, j: (i, 0, j)),
                  pl.BlockSpec((1, 1, tn), lambda i, j: (i, 0, j))],
        out_specs=pl.BlockSpec((1, C_ROWS, tn), lambda i, j: (i, 0, j)),
        out_shape=jax.ShapeDtypeStruct((depth, C_ROWS, n), F32),
        compiler_params=_cparams(("arbitrary", "arbitrary"), STREAM_VMEM_LIMIT),
        name="adaln",
    )(cvec, adaln_w, adaln_b.reshape(depth, 1, n))


def _qkv_kernel(*refs, rope, n_q, n_norm, v_add):
    x_ref, mod_ref, g_ref, w_ref, gain_ref, bd_ref = refs[:6]
    pos = 6
    if rope:
        cos = refs[pos][...]
        sin = refs[pos + 1][...]
        pos += 2
    if v_add:
        va_ref = refs[pos]
        pos += 1
    q_ref, k_ref, v_ref = refs[pos:]
    n_total = w_ref.shape[1]

    x = x_ref[0]
    sh = mod_ref[0, 0:1, :]
    sc = mod_ref[0, 1:2, :]
    hn = ((_rms(x) * g_ref[...]) * (1.0 + sc) + sh).astype(BF16)

    def project(c0):
        return jnp.dot(hn, w_ref[:, c0:c0 + MXU_W], preferred_element_type=F32)

    def mean_sq(z):
        return jnp.dot((z * z).astype(BF16), bd_ref[...], preferred_element_type=F32)

    def finish(c0, z, msq):
        if c0 < n_norm:
            z = z * lax.rsqrt(msq + EPS) * gain_ref[:, c0:c0 + MXU_W]
            if rope:
                z = jnp.concatenate(
                    [z[:, s0:s0 + LANES] * cos + pltpu.roll(z[:, s0:s0 + LANES], HEAD_DIM, 1) * sin
                     for s0 in range(0, MXU_W, LANES)], axis=-1)
            if c0 < n_q:
                q_ref[0, :, c0:c0 + MXU_W] = z.astype(BF16)
            else:
                k_ref[0, :, c0 - n_q:c0 - n_q + MXU_W] = z.astype(BF16)
        else:
            if v_add:
                z = z + va_ref[:, c0 - n_norm:c0 - n_norm + MXU_W]
            v_ref[0, :, c0 - n_norm:c0 - n_norm + MXU_W] = z.astype(BF16)

    starts = list(range(0, n_total, MXU_W))
    zs, ms = {}, {}
    for step in range(len(starts) + 2):
        if step < len(starts):
            zs[step] = project(starts[step])
        if 0 <= step - 1 < len(starts) and starts[step - 1] < n_norm:
            ms[step - 1] = mean_sq(zs[step - 1])
        if 0 <= step - 2 < len(starts):
            finish(starts[step - 2], zs.pop(step - 2), ms.pop(step - 2, None))


def _qkv_proj(x, mod, g, w, gain, bd, tabs, v_add, *, n_q, n_norm, tile):
    b, l, d = x.shape
    nmod = mod.shape[0]
    n = w.shape[1]
    n_k = n_norm - n_q
    n_v = n - n_norm
    rope = tabs is not None
    mod_map = (lambda bi, ti: (bi, 0, 0)) if nmod > 1 else (lambda bi, ti: (0, 0, 0))
    const2 = lambda bi, ti: (0, 0)
    row = lambda bi, ti: (bi, ti, 0)
    in_specs = [pl.BlockSpec((1, tile, d), row),
                pl.BlockSpec((1, MOD_ROWS, d), mod_map),
                pl.BlockSpec((1, d), const2),
                _resident((d, n), const2),
                pl.BlockSpec((1, n_norm), const2),
                pl.BlockSpec((MXU_W, MXU_W), const2)]
    args = [x, mod, g, w, gain, bd]
    if rope:
        in_specs += [pl.BlockSpec((tile, LANES), lambda bi, ti: (ti, 0))] * 2
        args += list(tabs)
    if v_add is not None:
        in_specs.append(pl.BlockSpec((1, n_v), const2))
        args.append(v_add)
    return pl.pallas_call(
        functools.partial(_qkv_kernel, rope=rope, n_q=n_q, n_norm=n_norm, v_add=v_add is not None),
        grid=(b, l // tile),
        in_specs=in_specs,
        out_specs=[pl.BlockSpec((1, tile, n_q), row), pl.BlockSpec((1, tile, n_k), row),
                   pl.BlockSpec((1, tile, n_v), row)],
        out_shape=[jax.ShapeDtypeStruct((b, l, n_q), BF16), jax.ShapeDtypeStruct((b, l, n_k), BF16),
                   jax.ShapeDtypeStruct((b, l, n_v), BF16)],
        compiler_params=_cparams(("arbitrary", "arbitrary")),
        name="qkv_proj",
    )(*args)


def _dot_nt(a, b):
    return lax.dot_general(a, b, (((1,), (1,)), ((), ())), preferred_element_type=F32)


def _row_max(s_list):
    m = s_list[0].max(axis=-1, keepdims=True)
    for s in s_list[1:]:
        m = jnp.maximum(m, s.max(axis=-1, keepdims=True))
    return m


def _split_heads(qp):
    mask_a = _head_a_mask()
    zero = jnp.zeros_like(qp)
    return jnp.where(mask_a, qp, zero), jnp.where(mask_a, zero, qp)


def _softmax_pv(s_list, v_list, sink_logit, dv):
    m = _row_max(s_list)
    if sink_logit is not None:
        m = jnp.maximum(m, sink_logit)
    o = None
    for s, v in zip(s_list, v_list):
        oj = jnp.dot(jnp.exp2(s - m).astype(BF16), v, preferred_element_type=F32)
        o = oj if o is None else o + oj
    den = o[:, dv:dv + 1]
    if sink_logit is not None:
        den = den + jnp.exp2(sink_logit - m)
    return o[:, :dv] * (1.0 / den)


def _head_tiles(q_ref, tq, per_trip, tile_ctx, logits, finish, emit, lookahead=1):
    nq = q_ref.shape[1] // tq
    assert nq % per_trip == 0
    n_heads = 2 * (q_ref.shape[2] // LANES)

    def heads(r0):
        hs = []
        for p0 in range(0, q_ref.shape[2], LANES):
            hs += list(_split_heads(q_ref[0, pl.ds(r0, tq), p0:p0 + LANES]))
        return hs

    def body(i, carry):
        pending = []
        outs = []

        def retire():
            r_done, ctx, s, g = pending.pop(0)
            outs.append(finish(ctx, s, g))
            if len(outs) == n_heads:
                emit(r_done, list(outs))
                outs.clear()

        for u in range(per_trip):
            r0 = pl.multiple_of((i * per_trip + u) * tq, tq)
            ctx = tile_ctx(r0)
            for g, q in enumerate(heads(r0)):
                pending.append((r0, ctx, logits(ctx, q, g), g))
                if len(pending) > lookahead:
                    retire()
        while pending:
            retire()
        return carry

    lax.fori_loop(0, nq // per_trip, body, 0)


def _diff_attn_kernel(*refs, lam_init, n_src, tq, per_trip):
    lam_ref, hg_ref, q_ref = refs[:3]
    kv_refs = refs[3:3 + 2 * n_src]
    o_ref = refs[3 + 2 * n_src]
    va_refs = refs[4 + 2 * n_src:]
    n_heads = q_ref.shape[2] // LANES
    lq = lam_ref[...]
    lam = (jnp.exp(jnp.sum(lq[0:1] * lq[1:2], axis=-1, keepdims=True))
           - jnp.exp(jnp.sum(lq[2:3] * lq[3:4], axis=-1, keepdims=True)) + lam_init)
    hg = hg_ref[...] * (1.0 - lam_init)
    lane = lax.broadcasted_iota(jnp.int32, (1, LANES), 1)
    for h in range(n_heads):
        for j in range(n_src):
            s = kv_refs[2 * j].shape[1]
            va = va_refs[h * n_src + j]
            va[:, :DIFF_V_DIM] = kv_refs[2 * j + 1][0, :, h * LANES:(h + 1) * LANES]
            va[:, DIFF_V_DIM:] = jnp.broadcast_to(
                jnp.where(lane == 0, 1.0, 0.0).astype(BF16), (s, LANES))

    def logits(ctx, q, g):
        h = g // 2
        return [_dot_nt(q, kv_refs[2 * j][0, :, h * LANES:(h + 1) * LANES]) for j in range(n_src)]

    def finish(ctx, s, g):
        h = g // 2
        return _softmax_pv(s, [va_refs[h * n_src + j][...] for j in range(n_src)], None,
                           DIFF_V_DIM)

    def emit(r0, outs):
        for h in range(n_heads):
            o = outs[2 * h] - lam * outs[2 * h + 1]
            o_ref[0, pl.ds(r0, tq), h * LANES:(h + 1) * LANES] = (_rms(o) * hg).astype(BF16)

    _head_tiles(q_ref, tq, per_trip, lambda r0: None, logits, finish, emit)


def _diff_attention(lam_p, head_g, q, kvs, *, lam_init, tq, per_trip, heads_per_step):
    b, lq, d = q.shape
    n_src = len(kvs)
    width = heads_per_step * LANES
    blk = lambda bi, h: (bi, 0, h)
    in_specs = [pl.BlockSpec((4, HEAD_DIM), lambda bi, h: (0, 0)),
                pl.BlockSpec((1, DIFF_V_DIM), lambda bi, h: (0, 0)),
                pl.BlockSpec((1, lq, width), blk)]
    args = [lam_p, head_g, q]
    for k, v in kvs:
        in_specs += [pl.BlockSpec((1, k.shape[1], width), blk)] * 2
        args += [k, v]
    return pl.pallas_call(
        functools.partial(_diff_attn_kernel, lam_init=lam_init, n_src=n_src, tq=tq,
                          per_trip=per_trip),
        grid=(b, DIFF_HEADS // heads_per_step),
        in_specs=in_specs,
        out_specs=pl.BlockSpec((1, lq, width), blk),
        out_shape=jax.ShapeDtypeStruct((b, lq, d), BF16),
        scratch_shapes=[pltpu.VMEM((k.shape[1], DIFF_V_DIM + LANES), BF16)
                        for _ in range(heads_per_step) for k, _ in kvs],
        compiler_params=_cparams(("arbitrary", "arbitrary")),
        name="diff_attn",
    )(*args)


def _gqa_emit(o_ref, tq):
    def emit(r0, outs):
        o_ref[0, pl.ds(r0, tq), :] = jnp.concatenate(outs, axis=-1).astype(BF16)
    return emit


def _gqa_dense_kernel(*refs, n_src, tq, per_trip, sink):
    if sink:
        sink_ref = refs[0]
        refs = refs[1:]
    q_ref = refs[0]
    kv_refs = refs[1:1 + 2 * n_src]
    o_ref = refs[1 + 2 * n_src]
    kk = pl.program_id(1)

    def logits(ctx, q, g):
        return [_dot_nt(q, kv_refs[2 * j][0]) for j in range(n_src)]

    def finish(ctx, s, g):
        v = [kv_refs[2 * j + 1][0] for j in range(n_src)]
        return _softmax_pv(s, v, sink_ref[kk, g] * LOG2E if sink else None, HEAD_DIM)

    _head_tiles(q_ref, tq, per_trip, lambda r0: None, logits, finish, _gqa_emit(o_ref, tq))


def _gqa_dense_attention(q, kvs, *, tq, per_trip, sink=None):
    b, lq, d = q.shape
    n_src = len(kvs)
    in_specs = []
    args = []
    if sink is not None:
        in_specs.append(pl.BlockSpec(memory_space=pltpu.SMEM))
        args.append(sink)
    blk = lambda bi, kk: (bi, 0, kk)
    q_width = GQA_GROUP * HEAD_DIM
    in_specs.append(pl.BlockSpec((1, lq, q_width), blk))
    args.append(q)
    for k, v in kvs:
        in_specs += [pl.BlockSpec((1, k.shape[1], LANES), blk)] * 2
        args += [k, v]
    return pl.pallas_call(
        functools.partial(_gqa_dense_kernel, n_src=n_src, tq=tq, per_trip=per_trip,
                          sink=sink is not None),
        grid=(b, GQA_KV_HEADS),
        in_specs=in_specs,
        out_specs=pl.BlockSpec((1, lq, q_width), blk),
        out_shape=jax.ShapeDtypeStruct((b, lq, d), BF16),
        compiler_params=_cparams(("arbitrary", "arbitrary")),
        name="gqa_dense_attn",
    )(*args)


def _gqa_window_kernel(sink_ref, q_ref, kl_ref, vl_ref, kc_ref, vc_ref, o_ref, *, tq, per_trip):
    n_lat = q_ref.shape[1]
    band = min(tq + 2 * WINDOW, n_lat)
    kk = pl.program_id(1)
    rel0 = (lax.broadcasted_iota(jnp.int32, (tq, band), 1)
            - lax.broadcasted_iota(jnp.int32, (tq, band), 0))

    def tile_ctx(r0):
        start = pl.multiple_of(jnp.clip(r0 - WINDOW, 0, n_lat - band), WINDOW)
        kb = kl_ref[0, pl.ds(start, band), :]
        vb = vl_ref[0, pl.ds(start, band), :]
        in_band = jnp.abs(rel0 + (start - r0)) <= WINDOW
        return kb, vb, in_band

    def logits(ctx, q, g):
        kb, _, in_band = ctx
        return [jnp.where(in_band, _dot_nt(q, kb), NEG_BIG), _dot_nt(q, kc_ref[0])]

    def finish(ctx, s, g):
        return _softmax_pv(s, [ctx[1], vc_ref[0]], sink_ref[kk, g] * LOG2E, HEAD_DIM)

    _head_tiles(q_ref, tq, per_trip, tile_ctx, logits, finish, _gqa_emit(o_ref, tq), lookahead=2)


def _gqa_window_attention(q, k_l, v_l, k_c, v_c, sink, *, tq, per_trip):
    b, lq, d = q.shape
    lc = k_c.shape[1]
    blk = lambda bi, kk: (bi, 0, kk)
    return pl.pallas_call(
        functools.partial(_gqa_window_kernel, tq=tq, per_trip=per_trip),
        grid=(b, GQA_KV_HEADS),
        in_specs=[pl.BlockSpec(memory_space=pltpu.SMEM),
                  pl.BlockSpec((1, lq, GQA_GROUP * HEAD_DIM), blk),
                  pl.BlockSpec((1, lq, LANES), blk),
                  pl.BlockSpec((1, lq, LANES), blk),
                  pl.BlockSpec((1, lc, LANES), blk),
                  pl.BlockSpec((1, lc, LANES), blk)],
        out_specs=pl.BlockSpec((1, lq, GQA_GROUP * HEAD_DIM), blk),
        out_shape=jax.ShapeDtypeStruct((b, lq, d), BF16),
        compiler_params=_cparams(("arbitrary", "arbitrary")),
        name="gqa_window_attn",
    )(sink, q, k_l, v_l, k_c, v_c)


def _tail_kernel(op_ref, o_ref, ox_ref, xp_ref, x_ref, xx_ref, mod_ref, g_ref, wo_ref, wu_ref,
                 cw_ref, cb_ref, wd_ref, out_ref, olhs_ref, xn_ref, lhs_ref, act_ref, *, tile, sub,
                 tiles_per_seq):
    ti = pl.program_id(1)
    oh = BF16_SUBLANES
    xh = SUBLANES
    m = tile + 2 * xh
    olhs_ref[0:oh, :] = op_ref[0]
    olhs_ref[oh:oh + tile, :] = o_ref[0]
    olhs_ref[oh + tile:, :] = ox_ref[0]
    y = jnp.dot(olhs_ref[...], wo_ref[0], preferred_element_type=F32)[oh - xh:oh + tile + xh]
    g1 = mod_ref[0, 2:3, :]
    sh2 = mod_ref[0, 3:4, :]
    sc2 = mod_ref[0, 4:5, :]
    g2 = mod_ref[0, 5:6, :]
    xn = jnp.concatenate([xp_ref[0], x_ref[0], xx_ref[0]], axis=0) + g1 * y
    xn_ref[...] = xn
    hn = (_rms(xn) * g_ref[...]) * (1.0 + sc2) + sh2
    row = lax.broadcasted_iota(jnp.int32, (m, 1), 0)
    keep = jnp.logical_and(jnp.logical_or(ti > 0, row >= xh),
                           jnp.logical_or(ti < tiles_per_seq - 1, row < xh + tile))
    lhs_ref[...] = jnp.where(keep, hn, 0.0).astype(BF16)

    ms = sub + 2 * xh
    n_sub = tile // sub

    def conv(u, c0):
        prev = pltpu.roll(u, 1, 0)[xh:xh + sub]
        nxt = pltpu.roll(u, ms - 1, 0)[xh:xh + sub]
        cur = u[xh:xh + sub]
        return (prev * cw_ref[0, 0:1, c0:c0 + FF_CHUNK] + cur * cw_ref[0, 1:2, c0:c0 + FF_CHUNK]
                + nxt * cw_ref[0, 2:3, c0:c0 + FF_CHUNK] + cb_ref[0, :, c0:c0 + FF_CHUNK])

    def up(s, j):
        lhs = lhs_ref[s * sub:s * sub + ms, :]
        cg = j * FF_CHUNK
        cv = D_FF + j * FF_CHUNK
        return (jnp.dot(lhs, wu_ref[0, :, cg:cg + FF_CHUNK], preferred_element_type=F32),
                jnp.dot(lhs, wu_ref[0, :, cv:cv + FF_CHUNK], preferred_element_type=F32))

    def gated(j, u):
        cg = j * FF_CHUNK
        gate = conv(u[0], cg)
        val = conv(u[1], D_FF + cg)
        return (gate * jax.nn.sigmoid(gate) * val).astype(BF16)

    n_chunks = D_FF // FF_CHUNK
    bounds = [round(n_chunks * p / DOWN_PARTS) for p in range(DOWN_PARTS + 1)]
    accs = [None] * n_sub
    us = {}

    def down(s, c_lo, c_hi):
        lo, hi = c_lo * FF_CHUNK, c_hi * FF_CHUNK
        part = jnp.dot(act_ref[s, :, lo:hi], wd_ref[0, lo:hi, :], preferred_element_type=F32)
        accs[s] = part if accs[s] is None else accs[s] + part

    for step in range(n_chunks + 1):
        for s in range(n_sub):
            if step < n_chunks:
                us[s, step] = up(s, step)
            if 0 <= step - 1 < n_chunks:
                j = step - 1
                act_ref[s, :, j * FF_CHUNK:(j + 1) * FF_CHUNK] = gated(j, us.pop((s, j)))
                if j + 1 in bounds[1:]:
                    down(s, bounds[bounds.index(j + 1) - 1], j + 1)
    for s in range(n_sub):
        out_ref[0, s * sub:(s + 1) * sub, :] = (xn_ref[xh + s * sub:xh + (s + 1) * sub, :]
                                                + g2 * accs[s])


def _tail(o, x, mod, g2n, w_o, wo_layer, w_up, conv_w, conv_b, w_down, layer, *, tile):
    b, l, d = x.shape
    nmod = mod.shape[0]
    nt = l // tile
    ob = tile // BF16_SUBLANES
    xb = tile // SUBLANES
    mod_map = (lambda bi, ti: (bi, 0, 0)) if nmod > 1 else (lambda bi, ti: (0, 0, 0))
    row = lambda bi, ti: (bi, ti, 0)
    lay = lambda bi, ti: (layer, 0, 0)
    lay_o = lambda bi, ti: (wo_layer, 0, 0)
    prev = lambda nb: (lambda bi, ti: (bi, jnp.maximum(ti * nb - 1, 0), 0))
    nxt = lambda nb: (lambda bi, ti: (bi, jnp.minimum((ti + 1) * nb, nt * nb - 1), 0))
    sub = min(TAIL_SUB, tile)
    return pl.pallas_call(
        functools.partial(_tail_kernel, tile=tile, sub=sub, tiles_per_seq=nt),
        grid=(b, nt),
        in_specs=[pl.BlockSpec((1, BF16_SUBLANES, d), prev(ob)),
                  pl.BlockSpec((1, tile, d), row),
                  pl.BlockSpec((1, BF16_SUBLANES, d), nxt(ob)),
                  pl.BlockSpec((1, SUBLANES, d), prev(xb)),
                  pl.BlockSpec((1, tile, d), row),
                  pl.BlockSpec((1, SUBLANES, d), nxt(xb)),
                  pl.BlockSpec((1, MOD_ROWS, d), mod_map),
                  pl.BlockSpec((1, d), lambda bi, ti: (0, 0)),
                  _resident((1, d, d), lay_o),
                  _resident((1, d, 2 * D_FF), lay),
                  pl.BlockSpec((1, CONV_W, 2 * D_FF), lay),
                  pl.BlockSpec((1, 1, 2 * D_FF), lay),
                  _resident((1, D_FF, d), lay)],
        out_specs=pl.BlockSpec((1, tile, d), row),
        out_shape=jax.ShapeDtypeStruct((b, l, d), F32),
        scratch_shapes=[pltpu.VMEM((tile + 2 * BF16_SUBLANES, d), BF16),
                        pltpu.VMEM((tile + 2 * SUBLANES, d), F32),
                        pltpu.VMEM((tile + 2 * SUBLANES, d), BF16),
                        pltpu.VMEM((tile // sub, sub, D_FF), BF16)],
        compiler_params=_cparams(("arbitrary", "arbitrary")),
        name="layer_tail",
    )(o, o, o, x, x, x, mod, g2n, w_o, w_up, conv_w, conv_b, w_down)


def _rope_tables(n_lat):
    t = jnp.arange(n_lat)
    row = (t // GRID_W).astype(F32)
    col = (t % GRID_W).astype(F32)
    n_freq = HEAD_DIM // 4
    inv_freq = ROPE_THETA ** (-jnp.arange(n_freq, dtype=F32) / n_freq)
    ang = jnp.concatenate([row[:, None] * inv_freq, col[:, None] * inv_freq], axis=-1)
    cos = jnp.cos(ang)
    sin = jnp.sin(ang)
    return (jnp.concatenate([cos, cos, cos, cos], axis=-1),
            jnp.concatenate([-sin, -sin, sin, sin], axis=-1))


def _pair_layout(w):
    lead = w.shape[:-1]
    w = w.reshape(lead + (-1, 2, 2, QUARTER))
    return jnp.swapaxes(w, -3, -2).reshape(lead + (-1,))


def _dup_layout(w):
    lead = w.shape[:-1]
    w = w.reshape(lead + (-1, 2, 1, QUARTER))
    return jnp.broadcast_to(w, w.shape[:-2] + (2, QUARTER)).reshape(lead + (-1,))


def _head_mean_matrix():
    lane = np.arange(MXU_W)
    head = (lane // LANES) * 2 + (lane // QUARTER) % 2
    return jnp.asarray((head[:, None] == head[None, :]).astype(np.float32) / HEAD_DIM, dtype=BF16)


def _prep_qkv(w_qkv, qk_g, diff):
    n_q = D_MODEL
    w_q = _pair_layout(w_qkv[:, :n_q])
    if diff:
        n_kv = D_MODEL
        w_k = _pair_layout(w_qkv[:, n_q:n_q + n_kv])
        w_v = w_qkv[:, n_q + n_kv:]
        v_add = None
    else:
        n_kv = GQA_KV_HEADS * HEAD_DIM
        w_k = _dup_layout(w_qkv[:, n_q:n_q + n_kv])
        w_v = w_qkv[:, n_q + n_kv:].reshape(D_MODEL, GQA_KV_HEADS, HEAD_DIM)
        w_v = jnp.pad(w_v, ((0, 0), (0, 0), (0, LANES - HEAD_DIM))).reshape(D_MODEL, -1)
        one_col = np.zeros((LANES,), np.float32)
        one_col[HEAD_DIM] = 1.0
        v_add = jnp.asarray(np.tile(one_col, GQA_KV_HEADS)).reshape(1, -1)
    w = jnp.concatenate([w_q, w_k, w_v], axis=1)
    gain = jnp.concatenate([jnp.tile(_dup_layout(qk_g[0] * Q_SCALE), n_q // LANES),
                            jnp.tile(_dup_layout(qk_g[1]), w_k.shape[1] // LANES)])
    return w, gain.reshape(1, -1), v_add, n_q, n_q + w_k.shape[1]


def kernel(x, c, ctx, c_ctx, adaln_w, adaln_b, norm1_g, norm2_g, ffn_w_up, ffn_conv_w, ffn_conv_b, ffn_w_down, a_w_qkv, a_qk_g, a_lambda, a_head_g, a_w_o, b_w_qkv, b_qk_g, b_w_o, c_w_qkv, c_qk_g, c_sink, c_w_o):
    b, n_lat, d = x.shape
    n_ctx = ctx.shape[1]
    depth = adaln_w.shape[0]
    assert d == D_MODEL and b + 1 <= C_ROWS

    tabs = _rope_tables(n_lat)
    bd = _head_mean_matrix()

    cvec = jnp.concatenate([c, c_ctx[None, :], jnp.zeros((C_ROWS - b - 1, d), F32)], axis=0)
    mod_all = _adaln(cvec, adaln_w, adaln_b).reshape(depth, C_ROWS, 6, d)
    mod_all = jnp.pad(mod_all, ((0, 0), (0, 0), (0, MOD_ROWS - 6), (0, 0)))

    w_up_all = _to_bf16(ffn_w_up)
    w_down_all = _to_bf16(ffn_w_down)
    conv_b_all = ffn_conv_b.reshape(depth, 1, 2 * D_FF)
    qkv_all = [_to_bf16(w) for w in (a_w_qkv, b_w_qkv, c_w_qkv)]
    wo_all = [_to_bf16(w) for w in (a_w_o, b_w_o, c_w_o)]
    qk_g_all = (a_qk_g, b_qk_g, c_qk_g)

    qkv_tile = min(512, n_lat)
    tail_tile = min(512, n_lat)
    tq_lat = min(256, n_lat)
    tq_big = min(512, n_lat)
    tq_ctx = min(256, n_ctx)
    trip = lambda lq, tq, want: max(p for p in (1, 2, 4, 8) if p <= want and (lq // tq) % p == 0)

    h_ctx = ctx
    for i in range(depth):
        last = i == depth - 1
        j = i // N_MIXERS
        kind = i % N_MIXERS
        mod_l = mod_all[i, :b]
        mod_c = mod_all[i, b:b + 1]
        g1n = norm1_g[i].reshape(1, d)
        g2n = norm2_g[i].reshape(1, d)
        w_all, gain, v_add, n_q, n_norm = _prep_qkv(qkv_all[kind][j], qk_g_all[kind][j], kind == 0)

        q_l, k_l, v_l = _qkv_proj(x, mod_l, g1n, w_all, gain, bd, tabs, v_add,
                                  n_q=n_q, n_norm=n_norm, tile=qkv_tile)
        q_c, k_c, v_c = _qkv_proj(h_ctx, mod_c, g1n, w_all, gain, bd, None, v_add,
                                  n_q=n_q, n_norm=n_norm, tile=n_ctx)

        o_c = None
        if kind == 0:
            lam_init = 0.8 - 0.6 * math.exp(-0.3 * i)
            hg = a_head_g[j].reshape(1, DIFF_V_DIM)
            o_l = _diff_attention(a_lambda[j], hg, q_l, [(k_l, v_l), (k_c, v_c)], lam_init=lam_init,
                                  tq=tq_lat, per_trip=trip(n_lat, tq_lat, 8), heads_per_step=1)
            if not last:
                o_c = _diff_attention(a_lambda[j], hg, q_c, [(k_c, v_c)], lam_init=lam_init,
                                      tq=tq_ctx // 2, per_trip=2, heads_per_step=DIFF_HEADS)
        elif kind == 1:
            o_l = _gqa_dense_attention(q_l, [(k_l, v_l), (k_c, v_c)], tq=tq_big,
                                       per_trip=trip(n_lat, tq_big, 2))
            if not last:
                o_c = _gqa_dense_attention(q_c, [(k_c, v_c)], tq=tq_ctx, per_trip=1)
        else:
            sink = c_sink[j].reshape(GQA_KV_HEADS, GQA_GROUP)
            o_l = _gqa_window_attention(q_l, k_l, v_l, k_c, v_c, sink, tq=tq_lat,
                                        per_trip=trip(n_lat, tq_lat, 2))
            if not last:
                o_c = _gqa_dense_attention(q_c, [(k_c, v_c)], tq=tq_ctx, per_trip=1, sink=sink)

        tail = functools.partial(_tail, g2n=g2n, w_o=wo_all[kind], wo_layer=j, w_up=w_up_all,
                                 conv_w=ffn_conv_w, conv_b=conv_b_all, w_down=w_down_all, layer=i)
        x = tail(o_l, x, mod_l, tile=tail_tile)
        if not last:
            h_ctx = tail(o_c, h_ctx, mod_c, tile=n_ctx)
    return x
```
